```python
import math
import jax
import jax.numpy as jnp
from jax import lax
import numpy as np

D_MODEL = 1024
BATCH = 8
SEQ = 4096
DEPTH = 1

CTX_LEN = 256
GRID_W = 64
N_HEADS = 8
HEAD_DIM = 64
D_ATTN = N_HEADS * 2 * HEAD_DIM
D_HYENA = D_MODEL
SHORT_CONV = 3
FILTER_EMB = 33
FILTER_ORDER = 64
DECAY_TARGET = 1e-2
FAST_DECAY_PCT = 0.3
SLOW_DECAY_PCT = 1.5
D_FF = 2816
ROPE_THETA = 10000.0
Q_BLOCK = 128
N_MOD = 9
N_PROJ = 3 * D_HYENA + 3 * D_ATTN + 2 * D_MODEL
SPLITS = (3 * D_HYENA, 3 * D_HYENA + D_ATTN, 3 * D_HYENA + 2 * D_ATTN, 3 * D_HYENA + 3 * D_ATTN)
EPS = 1e-6

kernel_name = 'hybrid_hyena_diffattn_macaron_dit_layer'


def rmsnorm(x, g):
    xf = x.astype(jnp.float32)
    y = xf * lax.rsqrt(jnp.mean(xf * xf, axis=-1, keepdims=True) + EPS)
    return (y * g.astype(jnp.float32)).astype(x.dtype)


def modulate(x, shift, scale):
    return x * (1.0 + scale) + shift


def swiglu(x, w_in, w_out):
    gate, up = jnp.split(x @ w_in, 2, axis=-1)
    return (jax.nn.silu(gate) * up) @ w_out


def ffn_half(s, shift, scale, gate, g_pre, g_post, w_in, w_out):
    h = swiglu(modulate(rmsnorm(s, g_pre), shift, scale), w_in, w_out)
    return s + 0.5 * gate * rmsnorm(h, g_post)


def short_conv(x, w, b):
    L = x.shape[1]
    pad = SHORT_CONV // 2
    xp = jnp.pad(x, ((0, 0), (pad, pad), (0, 0)))
    return sum(xp[:, j:j + L] * w[j] for j in range(SHORT_CONV)) + b


def hyena_kernel(L, w1, b1, w2, b2, w3, b3, w4, freq):
    f32 = jnp.float32
    bands = (FILTER_EMB - 1) // 2
    t = jnp.linspace(0.0, 1.0, L, dtype=f32)[:, None]
    w = 2.0 * math.pi * jnp.arange(L, dtype=f32)[:, None] / L
    f = jnp.linspace(1e-4, bands - 1, bands, dtype=f32)[None, :]
    z = jnp.concatenate([t, jnp.cos(f * w), -jnp.sin(f * w)], axis=-1)
    h = jnp.sin(freq * (z @ w1 + b1))
    h = jnp.sin(freq * (h @ w2 + b2))
    h = jnp.sin(freq * (h @ w3 + b3))
    h = (h @ w4).astype(f32)
    deltas = jnp.linspace(math.log(DECAY_TARGET) / SLOW_DECAY_PCT,
                          math.log(DECAY_TARGET) / FAST_DECAY_PCT, D_HYENA, dtype=f32)
    decay = jnp.exp(-t * jnp.abs(deltas))
    h_fwd, h_bwd = jnp.split(h, 2, axis=-1)
    h_fwd = h_fwd * decay
    h_bwd = h_bwd * decay
    k = jnp.concatenate([h_fwd, jnp.zeros_like(h_fwd[:1]), h_bwd[:0:-1]], axis=0)
    return k / (jnp.sum(jnp.abs(k), axis=0, keepdims=True) + EPS)


def bidir_long_conv(u, k):
    L = u.shape[1]
    uf = jnp.fft.rfft(u.astype(jnp.float32), n=2 * L, axis=1)
    kf = jnp.fft.rfft(k, n=2 * L, axis=0)
    y = jnp.fft.irfft(uf * kf[None], n=2 * L, axis=1)[:, :L]
    return y.astype(u.dtype)


def hyena_branch(z_hy, conv_w, conv_b, filt, hy_bias):
    z = short_conv(z_hy, conv_w, conv_b)
    x0, x1, v = jnp.split(z, 3, axis=-1)
    k = hyena_kernel(z.shape[1], *filt)
    u = v * x1
    y = bidir_long_conv(u, k) + u * hy_bias
    return y * x0


def axial_rope_tables(L):
    rows = L // GRID_W
    r, col = jnp.meshgrid(jnp.arange(rows), jnp.arange(GRID_W), indexing='ij')
    r = r.reshape(-1).astype(jnp.float32)
    col = col.reshape(-1).astype(jnp.float32)
    half = HEAD_DIM // 2
    inv = ROPE_THETA ** (-jnp.arange(0, half, 2, dtype=jnp.float32) / half)
    ang = jnp.stack([r[:, None] * inv, col[:, None] * inv], axis=1)
    return jnp.cos(ang), jnp.sin(ang)


def apply_axial_rope(x, cos, sin):
    B, L = x.shape[:2]
    xr = x.reshape(B, L, N_HEADS, 2, 2, 2, HEAD_DIM // 4)
    x0, x1 = xr[..., 0, :], xr[..., 1, :]
    c = cos[None, :, None, None]
    s = sin[None, :, None, None]
    out = jnp.stack([x0 * c - x1 * s, x0 * s + x1 * c], axis=-2)
    return out.reshape(x.shape).astype(x.dtype)


def diff_attention(q, k, v, lam):
    B, L = q.shape[:2]
    nb = L // Q_BLOCK
    qb = jnp.moveaxis(q.reshape(B, nb, Q_BLOCK, N_HEADS, 2, HEAD_DIM), 1, 0)

    def block(qblk):
        s = jnp.einsum('bqhmd,bkhmd->bhmqk', qblk, k).astype(jnp.float32) * (HEAD_DIM ** -0.5)
        p = jax.nn.softmax(s, axis=-1)
        p = p[:, :, 0] - lam * p[:, :, 1]
        return jnp.einsum('bhqk,bkhe->bqhe', p.astype(v.dtype), v)

    o = lax.map(block, qb)
    return jnp.moveaxis(o, 0, 1).reshape(B, L, N_HEADS, 2 * HEAD_DIM)


def context_kv(h, w_in):
    B, L = h.shape[:2]
    k, v = jnp.split(h @ w_in[:, SPLITS[1]:SPLITS[3]], 2, axis=-1)
    return k.reshape(B, L, N_HEADS, 2, HEAD_DIM), v.reshape(B, L, N_HEADS, 2 * HEAD_DIM)


def token_mixer(h, w_in, conv_w, conv_b, filt, hy_bias, lam, lam_init, subln_g,
                w_hy_out, w_da_out, w_o, rope, ctx_kv):
    B, L = h.shape[:2]
    z_hy, q, k, v, gates = jnp.split(h @ w_in, SPLITS, axis=-1)
    y_hy = hyena_branch(z_hy, conv_w, conv_b, filt, hy_bias) @ w_hy_out
    q = q.reshape(B, L, N_HEADS, 2, HEAD_DIM)
    k = k.reshape(B, L, N_HEADS, 2, HEAD_DIM)
    v = v.reshape(B, L, N_HEADS, 2 * HEAD_DIM)
    if rope is not None:
        q = apply_axial_rope(q, *rope)
        k = apply_axial_rope(k, *rope)
    if ctx_kv is not None:
        k = jnp.concatenate([k, ctx_kv[0]], axis=1)
        v = jnp.concatenate([v, ctx_kv[1]], axis=1)
    o = diff_attention(q, k, v, lam)
    o = rmsnorm(o, subln_g) * (1.0 - lam_init)
    y_da = o.reshape(B, L, D_ATTN) @ w_da_out
    g_hy, g_da = jnp.split(jax.nn.sigmoid(gates), 2, axis=-1)
    return (g_hy * y_hy + g_da * y_da) @ w_o


def setup_inputs(seed: int = 0) -> dict:
    key = jax.random.key(seed)
    ks = jax.random.split(key, 32)
    f32 = jnp.float32

    def nrm(k, shape, scale):
        return jax.random.normal(k, shape, f32) * scale

    return {
        'x': nrm(ks[0], (BATCH, SEQ, D_MODEL), 1.0),
        'c': nrm(ks[1], (BATCH, D_MODEL), 1.0),
        'ctx': nrm(ks[2], (BATCH, CTX_LEN, D_MODEL), 1.0),
        'c_ctx': nrm(ks[3], (D_MODEL,), 1.0),
        'w_ada': nrm(ks[4], (DEPTH, D_MODEL, N_MOD * D_MODEL), 0.5 * D_MODEL ** -0.5),
        'b_ada': nrm(ks[5], (DEPTH, N_MOD * D_MODEL), 0.01),
        'norm_g': 1.0 + nrm(ks[6], (DEPTH, 6, D_MODEL), 0.02),
        'w_ff_in': nrm(ks[7], (DEPTH, 2, D_MODEL, 2 * D_FF), D_MODEL ** -0.5),
        'w_ff_out': nrm(ks[8], (DEPTH, 2, D_FF, D_MODEL), D_FF ** -0.5),
        'w_in': nrm(ks[9], (DEPTH, D_MODEL, N_PROJ), D_MODEL ** -0.5),
        'hy_conv_w': nrm(ks[10], (DEPTH, SHORT_CONV, 3 * D_HYENA), SHORT_CONV ** -0.5),
        'hy_conv_b': nrm(ks[11], (DEPTH, 3 * D_HYENA), 0.01),
        'filt_w1': nrm(ks[12], (DEPTH, FILTER_EMB, FILTER_ORDER), FILTER_EMB ** -0.5),
        'filt_b1': nrm(ks[13], (DEPTH, FILTER_ORDER), 0.1),
        'filt_w2': nrm(ks[14], (DEPTH, FILTER_ORDER, FILTER_ORDER), FILTER_ORDER ** -0.5),
        'filt_b2': nrm(ks[15], (DEPTH, FILTER_ORDER), 0.1),
        'filt_w3': nrm(ks[16], (DEPTH, FILTER_ORDER, FILTER_ORDER), FILTER_ORDER ** -0.5),
        'filt_b3': nrm(ks[17], (DEPTH, FILTER_ORDER), 0.1),
        'filt_w4': nrm(ks[18], (DEPTH, FILTER_ORDER, 2 * D_HYENA), FILTER_ORDER ** -0.5),
        'filt_freq': 1.0 + nrm(ks[19], (DEPTH, FILTER_ORDER), 0.01),
        'hy_bias': nrm(ks[20], (DEPTH, D_HYENA), 0.1),
        'lambda_q1': nrm(ks[21], (DEPTH, HEAD_DIM), 0.1),
        'lambda_k1': nrm(ks[22], (DEPTH, HEAD_DIM), 0.1),
        'lambda_q2': nrm(ks[23], (DEPTH, HEAD_DIM), 0.1),
        'lambda_k2': nrm(ks[24], (DEPTH, HEAD_DIM), 0.1),
        'subln_g': 1.0 + nrm(ks[25], (DEPTH, 2 * HEAD_DIM), 0.02),
        'w_hy_out': nrm(ks[26], (DEPTH, D_HYENA, D_MODEL), D_HYENA ** -0.5),
        'w_da_out': nrm(ks[27], (DEPTH, D_ATTN, D_MODEL), D_ATTN ** -0.5),
        'w_o': nrm(ks[28], (DEPTH, D_MODEL, D_MODEL), D_MODEL ** -0.5),
    }


def reference(x, c, ctx, c_ctx, w_ada, b_ada, norm_g, w_ff_in, w_ff_out, w_in,
              hy_conv_w, hy_conv_b, filt_w1, filt_b1, filt_w2, filt_b2, filt_w3, filt_b3,
              filt_w4, filt_freq, hy_bias, lambda_q1, lambda_k1, lambda_q2, lambda_k2,
              subln_g, w_hy_out, w_da_out, w_o):
    rope = axial_rope_tables(x.shape[1])
    xs, cs = x, ctx
    for l in range(DEPTH):
        lam_init = 0.8 - 0.6 * math.exp(-0.3 * l)
        lam = (jnp.exp(jnp.sum(lambda_q1[l] * lambda_k1[l]).astype(jnp.float32))
               - jnp.exp(jnp.sum(lambda_q2[l] * lambda_k2[l]).astype(jnp.float32)) + lam_init)
        mod_x = jnp.split((jax.nn.silu(c) @ w_ada[l] + b_ada[l])[:, None, :], N_MOD, axis=-1)
        mod_c = jnp.split((jax.nn.silu(c_ctx) @ w_ada[l] + b_ada[l])[None, None, :], N_MOD, axis=-1)
        g = norm_g[l]
        filt = (filt_w1[l], filt_b1[l], filt_w2[l], filt_b2[l], filt_w3[l], filt_b3[l],
                filt_w4[l], filt_freq[l])
        mixer_params = (w_in[l], hy_conv_w[l], hy_conv_b[l], filt, hy_bias[l], lam, lam_init,
                        subln_g[l], w_hy_out[l], w_da_out[l], w_o[l])
        xs = ffn_half(xs, mod_x[0], mod_x[1], mod_x[2], g[0], g[1], w_ff_in[l, 0], w_ff_out[l, 0])
        cs = ffn_half(cs, mod_c[0], mod_c[1], mod_c[2], g[0], g[1], w_ff_in[l, 0], w_ff_out[l, 0])
        hx = modulate(rmsnorm(xs, g[2]), mod_x[3], mod_x[4])
        hc = modulate(rmsnorm(cs, g[2]), mod_c[3], mod_c[4])
        y = token_mixer(hx, *mixer_params, rope, context_kv(hc, w_in[l]))
        xs = xs + mod_x[5] * rmsnorm(y, g[3])
        if l < DEPTH - 1:
            yc = token_mixer(hc, *mixer_params, None, None)
            cs = cs + mod_c[5] * rmsnorm(yc, g[3])
            cs = ffn_half(cs, mod_c[6], mod_c[7], mod_c[8], g[4], g[5], w_ff_in[l, 1], w_ff_out[l, 1])
        xs = ffn_half(xs, mod_x[6], mod_x[7], mod_x[8], g[4], g[5], w_ff_in[l, 1], w_ff_out[l, 1])
    return xs
```

```python
import functools
import math

import jax
import jax.numpy as jnp
from jax import lax
from jax.experimental import pallas as pl
from jax.experimental.pallas import tpu as pltpu

F32 = jnp.float32
BF16 = jnp.bfloat16
HIGHEST = lax.Precision.HIGHEST

N_HEADS = 8
HEAD_DIM = 64
GRID_W = 64
ROPE_THETA = 10000.0
SHORT_CONV = 3
FILTER_EMB = 33
DECAY_TARGET = 1e-2
FAST_DECAY_PCT = 0.3
SLOW_DECAY_PCT = 1.5
EPS = 1e-6
N_MOD = 9

LANES = 128
DFT_N2 = 128
VMEM_LIMIT = 56 * 1024 * 1024


def _params(sem, vmem=VMEM_LIMIT):
    return pltpu.CompilerParams(dimension_semantics=sem, vmem_limit_bytes=vmem)


def _rms(x, g):
    return x * lax.rsqrt(jnp.mean(x * x, axis=-1, keepdims=True) + EPS) * g


def _dot(a, b, **kw):
    return jnp.dot(a, b, preferred_element_type=F32, **kw)


def _ada_kernel(c_ref, w_ref, b_ref, o_ref):
    c = c_ref[...]
    a = c * jax.nn.sigmoid(c)
    o_ref[...] = _dot(a, w_ref[...], precision=HIGHEST) + b_ref[...]


def _ada(c16, w_ada, b_ada):
    d = c16.shape[1]
    n = w_ada.shape[1]
    return pl.pallas_call(
        _ada_kernel,
        grid=(n // d,),
        in_specs=[pl.BlockSpec((16, d), lambda j: (0, 0)),
                  pl.BlockSpec((d, d), lambda j: (0, j)),
                  pl.BlockSpec((1, d), lambda j: (0, j))],
        out_specs=pl.BlockSpec((16, d), lambda j: (0, j)),
        out_shape=jax.ShapeDtypeStruct((16, n), F32),
        compiler_params=_params(("arbitrary",)),
        name="ada",
    )(c16, w_ada, b_ada.reshape(1, n))


def _ffn_kernel(mrow, grow, s_ref, mod_ref, g_ref, wg_ref, wu_ref, wo_ref, o_ref, xn_ref, acc_ref):
    j = pl.program_id(2)

    @pl.when(j == 0)
    def _():
        y = _rms(s_ref[0], g_ref[grow:grow + 1, :])
        shift = mod_ref[0, mrow:mrow + 1, :]
        scale = mod_ref[0, mrow + 1:mrow + 2, :]
        xn_ref[...] = (y * (1.0 + scale) + shift).astype(BF16)

    xn = xn_ref[...]
    hg = _dot(xn, wg_ref[...])
    hu = _dot(xn, wu_ref[...])
    h = (hg * jax.nn.sigmoid(hg) * hu).astype(BF16)
    part = _dot(h, wo_ref[...])

    @pl.when(j == 0)
    def _():
        acc_ref[...] = part

    @pl.when(j > 0)
    def _():
        acc_ref[...] += part

    @pl.when(j == pl.num_programs(2) - 1)
    def _():
        r = _rms(acc_ref[...], g_ref[grow + 1:grow + 2, :])
        gate = mod_ref[0, mrow + 2:mrow + 3, :]
        o_ref[0] = s_ref[0] + 0.5 * gate * r


def _ffn(s, mod, g, w_in, w_out, *, mrow, grow, mod_row_of_batch, tm, n_ff_chunks=2):
    b, t, d = s.shape
    dff = w_out.shape[0]
    tf = dff // n_ff_chunks
    return pl.pallas_call(
        functools.partial(_ffn_kernel, mrow, grow),
        grid=(b, t // tm, n_ff_chunks),
        in_specs=[pl.BlockSpec((1, tm, d), lambda bi, i, j: (bi, i, 0)),
                  pl.BlockSpec((1, N_MOD, d), lambda bi, i, j: (mod_row_of_batch(bi), 0, 0)),
                  pl.BlockSpec(g.shape, lambda bi, i, j: (0, 0)),
                  pl.BlockSpec((d, tf), lambda bi, i, j: (0, j)),
                  pl.BlockSpec((d, tf), lambda bi, i, j: (0, n_ff_chunks + j)),
                  pl.BlockSpec((tf, d), lambda bi, i, j: (j, 0))],
        out_specs=pl.BlockSpec((1, tm, d), lambda bi, i, j: (bi, i, 0)),
        out_shape=jax.ShapeDtypeStruct((b, t, d), F32),
        scratch_shapes=[pltpu.VMEM((tm, d), BF16), pltpu.VMEM((tm, d), F32)],
        compiler_params=_params(("parallel", "parallel", "arbitrary")),
        name="ffn",
    )(s, mod, g, w_in, w_in, w_out)


def _rope(acc, cos, sa, sb):
    cols = []
    for k in range(acc.shape[1] // LANES):
        blk = acc[:, k * LANES:(k + 1) * LANES]
        cols.append(blk * cos + pltpu.roll(blk, 16, 1) * sa + pltpu.roll(blk, LANES - 16, 1) * sb)
    return jnp.concatenate(cols, axis=1)


def _proj_kernel(modes, mrow, grow, use_rope, x_ref, mod_ref, g_ref, w_ref, *rest):
    if use_rope:
        cos_ref, sa_ref, sb_ref, o_ref, xn_ref = rest
    else:
        o_ref, xn_ref = rest
    j = pl.program_id(2)

    @pl.when(j == 0)
    def _():
        y = _rms(x_ref[0], g_ref[grow:grow + 1, :])
        shift = mod_ref[0, mrow:mrow + 1, :]
        scale = mod_ref[0, mrow + 1:mrow + 2, :]
        xn_ref[...] = (y * (1.0 + scale) + shift).astype(BF16)

    acc = _dot(xn_ref[...], w_ref[...])

    def chunk_is(mode):
        hit = None
        for jj, m in enumerate(modes):
            if m == mode:
                hit = (j == jj) if hit is None else (hit | (j == jj))
        return hit

    if "plain" in modes:
        @pl.when(chunk_is("plain"))
        def _():
            o_ref[0] = acc.astype(BF16)

    if "sigmoid" in modes:
        @pl.when(chunk_is("sigmoid"))
        def _():
            o_ref[0] = jax.nn.sigmoid(acc).astype(BF16)

    if "rope_q" in modes:
        @pl.when(chunk_is("rope_q"))
        def _():
            r = _rope(acc, cos_ref[...], sa_ref[...], sb_ref[...])
            o_ref[0] = (r * (HEAD_DIM ** -0.5)).astype(BF16)

    if "rope_k" in modes:
        @pl.when(chunk_is("rope_k"))
        def _():
            o_ref[0] = _rope(acc, cos_ref[...], sa_ref[...], sb_ref[...]).astype(BF16)


def _proj(x, mod, g, w, *, modes, col_chunk0, mrow, grow, mod_row_of_batch, tm, rope=None):
    b, t, d = x.shape
    tn = d
    nch = len(modes)
    in_specs = [pl.BlockSpec((1, tm, d), lambda bi, i, j: (bi, i, 0)),
                pl.BlockSpec((1, N_MOD, d), lambda bi, i, j: (mod_row_of_batch(bi), 0, 0)),
                pl.BlockSpec(g.shape, lambda bi, i, j: (0, 0)),
                pl.BlockSpec((d, tn), lambda bi, i, j: (0, col_chunk0 + j))]
    args = [x, mod, g, w]
    if rope is not None:
        in_specs += [pl.BlockSpec((tm, LANES), lambda bi, i, j: (i, 0))] * 3
        args += list(rope)
    return pl.pallas_call(
        functools.partial(_proj_kernel, modes, mrow, grow, rope is not None),
        grid=(b, t // tm, nch),
        in_specs=in_specs,
        out_specs=pl.BlockSpec((1, tm, tn), lambda bi, i, j: (bi, i, j)),
        out_shape=jax.ShapeDtypeStruct((b, t, nch * tn), BF16),
        scratch_shapes=[pltpu.VMEM((tm, d), BF16)],
        compiler_params=_params(("parallel", "parallel", "arbitrary")),
        name="proj",
    )(*args)


def _rope_tables(seq):
    pos = jnp.arange(seq)
    r = (pos // GRID_W).astype(F32)[:, None]
    col = (pos % GRID_W).astype(F32)[:, None]
    half = HEAD_DIM // 2
    inv = ROPE_THETA ** (-jnp.arange(0, half, 2, dtype=F32) / half)
    cr, sr = jnp.cos(r * inv), jnp.sin(r * inv)
    cc, sc = jnp.cos(col * inv), jnp.sin(col * inv)
    z = jnp.zeros_like(sr)
    cos = jnp.concatenate([cr, cr, cc, cc], axis=1)
    sa = jnp.concatenate([z, sr, z, sc], axis=1)
    sb = jnp.concatenate([-sr, z, -sc, z], axis=1)
    rep = LANES // HEAD_DIM
    return tuple(jnp.tile(t, (1, rep)) for t in (cos, sa, sb))


def _dft_tables(seq):
    n = 2 * seq
    n2 = DFT_N2
    n1 = n // n2
    k1h = n1 // 2

    def cs(num, den):
        ang = (2.0 * math.pi / den) * (num % den).astype(F32)
        return jnp.cos(ang), jnp.sin(ang)

    i1 = jnp.arange(n1)
    c, s = cs(i1[:, None] * i1[None, :], n1)
    f1_data = jnp.concatenate([jnp.concatenate([c[:, :k1h], s[:, :k1h]], 1),
                               jnp.concatenate([-s[:, :k1h], c[:, :k1h]], 1)], 0)
    f1_real = jnp.concatenate([c, -s], 0)
    ci, si = c[:k1h, :] / n, s[:k1h, :] / n
    f1_inv = jnp.concatenate([jnp.concatenate([ci, -si], 1),
                              jnp.concatenate([si, ci], 1)], 0)
    i2 = jnp.arange(n2)
    num = i2[None, None, :] * (i1[:, None, None] + n1 * i2[None, :, None])
    c2, s2 = cs(num, n)
    g = jnp.concatenate([jnp.concatenate([c2, s2], 2),
                         jnp.concatenate([-s2, c2], 2)], 1)
    h = jnp.swapaxes(g, 1, 2)
    return (f1_data.astype(BF16), f1_real.astype(BF16), f1_inv.astype(BF16),
            g.astype(BF16), h.astype(BF16))


def _filt_mlp_kernel(z_ref, w1_ref, b1_ref, w2_ref, b2_ref, w3_ref, b3_ref, fr_ref, o_ref):
    fr = fr_ref[...]
    h = jnp.sin(fr * (_dot(z_ref[...], w1_ref[...], precision=HIGHEST) + b1_ref[...]))
    h = jnp.sin(fr * (_dot(h, w2_ref[...], precision=HIGHEST) + b2_ref[...]))
    o_ref[...] = jnp.sin(fr * (_dot(h, w3_ref[...], precision=HIGHEST) + b3_ref[...]))


def _filt_spec_kernel(seq, n1, h_ref, w4f_ref, w4b_ref, t_ref, dl_ref, f1_ref, g_ref, o_ref, k_ref, r_ref):
    n2 = DFT_N2
    n = 2 * seq
    ct = o_ref.shape[2]
    h3 = h_ref[...]
    row = lax.broadcasted_iota(jnp.int32, (n, ct), 0)
    hk = jnp.where(row < seq, _dot(h3, w4f_ref[...], precision=HIGHEST),
                   _dot(h3, w4b_ref[...], precision=HIGHEST))
    kraw = jnp.where(row == seq, 0.0, hk * jnp.exp(-t_ref[...] * dl_ref[...]))
    k_ref[...] = kraw / (jnp.sum(jnp.abs(kraw), axis=0, keepdims=True) + EPS)

    f1 = f1_ref[...]

    def stage1(m, carry):
        z = k_ref[pl.ds(m, n1, stride=n2), :].astype(BF16)
        r_ref[pl.ds(pl.multiple_of(m * 2 * n1, 2 * n1), 2 * n1), :] = _dot(f1, z)
        return carry

    lax.fori_loop(0, n2, stage1, 0)

    def stage2(q, carry):
        a = jnp.concatenate([r_ref[pl.ds(q, n2, stride=2 * n1), :],
                             r_ref[pl.ds(n1 + q, n2, stride=2 * n1), :]], axis=0).astype(BF16)
        o_ref[q] = _dot(g_ref[q], a).astype(o_ref.dtype)
        return carry

    lax.fori_loop(0, n1, stage2, 0)


def _hyena_filter_spectrum(seq, d_hy, filt, tables, ct):
    w1, b1, w2, b2, w3, b3, w4, freq = filt
    f1_real, g = tables
    n = 2 * seq
    n2 = DFT_N2
    n1 = n // n2
    order = w1.shape[1]
    pad = LANES - order
    bands = (FILTER_EMB - 1) // 2
    t = jnp.linspace(0.0, 1.0, seq, dtype=F32)[:, None]
    w = 2.0 * math.pi * jnp.arange(seq, dtype=F32)[:, None] / seq
    f = jnp.linspace(1e-4, bands - 1, bands, dtype=F32)[None, :]
    z = jnp.concatenate([t, jnp.cos(f * w), -jnp.sin(f * w)], axis=-1)
    idx = jnp.concatenate([jnp.arange(seq), jnp.zeros((1,), jnp.int32), jnp.arange(seq - 1, 0, -1)])
    z_ext = jnp.pad(z[idx], ((0, 0), (0, LANES - FILTER_EMB)))
    t_ext = t[idx]
    deltas = jnp.abs(jnp.linspace(math.log(DECAY_TARGET) / SLOW_DECAY_PCT,
                                  math.log(DECAY_TARGET) / FAST_DECAY_PCT, d_hy, dtype=F32))[None, :]

    def padc(a):
        return jnp.pad(a, ((0, 0), (0, pad)))

    w1p = jnp.pad(w1, ((0, LANES - FILTER_EMB), (0, pad)))
    w2p = jnp.pad(w2, ((0, pad), (0, pad)))
    w3p = jnp.pad(w3, ((0, pad), (0, pad)))
    w4p = jnp.pad(w4, ((0, pad), (0, 0)))
    h3 = pl.pallas_call(
        _filt_mlp_kernel,
        out_shape=jax.ShapeDtypeStruct((n, LANES), F32),
        compiler_params=pltpu.CompilerParams(vmem_limit_bytes=VMEM_LIMIT),
        name="filt_mlp",
    )(z_ext, w1p, padc(b1[None]), w2p, padc(b2[None]), w3p, padc(b3[None]), padc(freq[None]))

    nct = d_hy // ct
    return pl.pallas_call(
        functools.partial(_filt_spec_kernel, seq, n1),
        grid=(nct,),
        in_specs=[pl.BlockSpec((n, LANES), lambda j: (0, 0)),
                  pl.BlockSpec((LANES, ct), lambda j: (0, j)),
                  pl.BlockSpec((LANES, ct), lambda j: (0, nct + j)),
                  pl.BlockSpec((n, 1), lambda j: (0, 0)),
                  pl.BlockSpec((1, ct), lambda j: (0, j)),
                  pl.BlockSpec(f1_real.shape, lambda j: (0, 0)),
                  pl.BlockSpec(g.shape, lambda j: (0, 0, 0))],
        out_specs=pl.BlockSpec((n1, 2 * n2, ct), lambda j: (0, 0, j)),
        out_shape=jax.ShapeDtypeStruct((n1, 2 * n2, d_hy), BF16),
        scratch_shapes=[pltpu.VMEM((n, ct), F32), pltpu.VMEM((n2 * 2 * n1, ct), F32)],
        compiler_params=_params(("arbitrary",)),
        name="filt_spec",
    )(h3, w4p, w4p, t_ext, deltas, f1_real, g)


def _conv3(z, w, b):
    seq = z.shape[0]
    row = lax.broadcasted_iota(jnp.int32, z.shape, 0)
    zm = jnp.where(row == 0, 0.0, pltpu.roll(z, 1, 0))
    zp = jnp.where(row == seq - 1, 0.0, pltpu.roll(z, seq - 1, 0))
    return zm * w[0:1, :] + z * w[1:2, :] + zp * w[2:3, :] + b


def _hyena_kernel(seq, n1, x0_ref, x1_ref, v_ref, cw0_ref, cw1_ref, cwv_ref, cb0_ref, cb1_ref, cbv_ref,
                  bias_ref, f1_ref, f1i_ref, g_ref, h_ref, kf_ref, o_ref, u_ref, r_ref):
    n2 = DFT_N2
    k1h = n1 // 2

    for bb in range(2):
        x1 = _conv3(x1_ref[bb].astype(F32), cw1_ref[...], cb1_ref[...])
        v = _conv3(v_ref[bb].astype(F32), cwv_ref[...], cbv_ref[...])
        u_ref[bb * seq:(bb + 1) * seq, :] = v * x1

    f1 = f1_ref[...]

    def stage1(m, carry):
        z = jnp.concatenate([u_ref[pl.ds(m, k1h, stride=n2), :],
                             u_ref[pl.ds(seq + m, k1h, stride=n2), :]], axis=0).astype(BF16)
        r_ref[pl.ds(pl.multiple_of(m * 2 * n1, 2 * n1), 2 * n1), :] = _dot(f1, z)
        return carry

    lax.fori_loop(0, n2, stage1, 0)

    def stage2(q, carry):
        re_rows = pl.ds(q, n2, stride=2 * n1)
        im_rows = pl.ds(n1 + q, n2, stride=2 * n1)
        a = jnp.concatenate([r_ref[re_rows, :], r_ref[im_rows, :]], axis=0).astype(BF16)
        x = _dot(g_ref[q], a)
        kf = kf_ref[q].astype(F32)
        xr, xi, kr, ki = x[:n2], x[n2:], kf[:n2], kf[n2:]
        y = jnp.concatenate([xr * kr - xi * ki, xr * ki + xi * kr], axis=0).astype(BF16)
        bq = _dot(h_ref[q], y)
        r_ref[re_rows, :] = bq[:n2]
        r_ref[im_rows, :] = bq[n2:]
        return carry

    lax.fori_loop(0, n1, stage2, 0)

    f1i = f1i_ref[...]
    bias = bias_ref[...]

    def stage3(m, carry):
        b = r_ref[pl.ds(pl.multiple_of(m * 2 * n1, 2 * n1), 2 * n1), :].astype(BF16)
        y = _dot(f1i, b)
        for bb in range(2):
            rows = pl.ds(bb * seq + m, k1h, stride=n2)
            u_ref[rows, :] = y[bb * k1h:(bb + 1) * k1h] + u_ref[rows, :] * bias
        return carry

    lax.fori_loop(0, n2, stage3, 0)

    for bb in range(2):
        x0 = _conv3(x0_ref[bb].astype(F32), cw0_ref[...], cb0_ref[...])
        o_ref[bb] = (u_ref[bb * seq:(bb + 1) * seq, :] * x0).astype(BF16)


def _hyena(p, conv_w, conv_b, hy_bias, kf, tables, d_hy, ct):
    b, seq, _ = p.shape
    f1_data, f1_inv, g, h = tables
    n2 = DFT_N2
    n1 = 2 * seq // n2
    nct = d_hy // ct
    once = pl.Buffered(1)
    zspec = lambda off: pl.BlockSpec((2, seq, ct), lambda j, q: (q, 0, off * nct + j))
    wspec = lambda off: pl.BlockSpec((SHORT_CONV, ct), lambda j, q: (0, off * nct + j))
    bspec = lambda off: pl.BlockSpec((1, ct), lambda j, q: (0, off * nct + j))
    cb = conv_b.reshape(1, -1)
    return pl.pallas_call(
        functools.partial(_hyena_kernel, seq, n1),
        grid=(nct, b // 2),
        in_specs=[zspec(0), zspec(1), zspec(2), wspec(0), wspec(1), wspec(2), bspec(0), bspec(1), bspec(2),
                  pl.BlockSpec((1, ct), lambda j, q: (0, j)),
                  pl.BlockSpec(f1_data.shape, lambda j, q: (0, 0)),
                  pl.BlockSpec(f1_inv.shape, lambda j, q: (0, 0)),
                  pl.BlockSpec(g.shape, lambda j, q: (0, 0, 0), pipeline_mode=once),
                  pl.BlockSpec(h.shape, lambda j, q: (0, 0, 0), pipeline_mode=once),
                  pl.BlockSpec((n1, 2 * n2, ct), lambda j, q: (0, 0, j), pipeline_mode=once)],
        out_specs=pl.BlockSpec((2, seq, ct), lambda j, q: (q, 0, j)),
        out_shape=jax.ShapeDtypeStruct((b, seq, d_hy), BF16),
        scratch_shapes=[pltpu.VMEM((2 * seq, ct), F32), pltpu.VMEM((n2 * 2 * n1, ct), F32)],
        compiler_params=_params(("arbitrary", "arbitrary")),
        name="hyena",
    )(p, p, p, conv_w, conv_w, conv_w, cb, cb, cb, hy_bias.reshape(1, -1), f1_data, f1_inv, g, h, kf)


def _attn_kernel(lam_init, q_ref, k_ref, v_ref, kc_ref, vc_ref, lam_ref, g_ref, o_ref):
    q = q_ref[0]
    k, v, kc, vc = k_ref[0], v_ref[0], kc_ref[0], vc_ref[0]
    lane = lax.broadcasted_iota(jnp.int32, q.shape, 1)
    nt = (((1,), (1,)), ((), ()))

    def softmax_pv(qm):
        s = lax.dot_general(qm, k, nt, preferred_element_type=F32)
        sc = lax.dot_general(qm, kc, nt, preferred_element_type=F32)
        m = jnp.maximum(jnp.max(s, axis=-1, keepdims=True), jnp.max(sc, axis=-1, keepdims=True))
        p = jnp.exp(s - m)
        pc = jnp.exp(sc - m)
        l = jnp.sum(p, axis=-1, keepdims=True) + jnp.sum(pc, axis=-1, keepdims=True)
        o = _dot(p.astype(BF16), v) + _dot(pc.astype(BF16), vc)
        return o / l

    o1 = softmax_pv(jnp.where(lane < HEAD_DIM, q, jnp.zeros_like(q)))
    o2 = softmax_pv(jnp.where(lane >= HEAD_DIM, q, jnp.zeros_like(q)))
    lp = lam_ref[...]
    lam = (jnp.exp(jnp.sum(lp[0:1] * lp[1:2], axis=-1, keepdims=True))
           - jnp.exp(jnp.sum(lp[2:3] * lp[3:4], axis=-1, keepdims=True)) + lam_init)
    o = o1 - lam * o2
    o_ref[0] = (_rms(o, g_ref[...]) * (1.0 - lam_init)).astype(BF16)


def _attn(p, pc, lam_rows, subln_g, lam_init, d_hy, tq):
    b, seq, _ = p.shape
    lctx = pc.shape[1]
    hw = 2 * HEAD_DIM
    c0 = 3 * d_hy // hw
    nh = N_HEADS
    return pl.pallas_call(
        functools.partial(_attn_kernel, lam_init),
        grid=(b, nh, seq // tq),
        in_specs=[pl.BlockSpec((1, tq, hw), lambda bi, hh, i: (bi, i, c0 + hh)),
                  pl.BlockSpec((1, seq, hw), lambda bi, hh, i: (bi, 0, c0 + nh + hh)),
                  pl.BlockSpec((1, seq, hw), lambda bi, hh, i: (bi, 0, c0 + 2 * nh + hh)),
                  pl.BlockSpec((1, lctx, hw), lambda bi, hh, i: (bi, 0, hh)),
                  pl.BlockSpec((1, lctx, hw), lambda bi, hh, i: (bi, 0, nh + hh)),
                  pl.BlockSpec(lam_rows.shape, lambda bi, hh, i: (0, 0)),
                  pl.BlockSpec((1, hw), lambda bi, hh, i: (0, 0))],
        out_specs=pl.BlockSpec((1, tq, hw), lambda bi, hh, i: (bi, i, hh)),
        out_shape=jax.ShapeDtypeStruct((b, seq, nh * hw), BF16),
        compiler_params=_params(("parallel", "parallel", "arbitrary")),
        name="attn",
    )(p, p, p, pc, pc, lam_rows, subln_g.reshape(1, hw))


def _merge_kernel(mrow, grow, s_ref, yh_ref, od_ref, gh_ref, gd_ref, mod_ref, g_ref,
                  why_ref, wda_ref, wo_ref, o_ref):
    y_hy = _dot(yh_ref[0], why_ref[...])
    y_da = _dot(od_ref[0], wda_ref[...])
    mix = gh_ref[0].astype(F32) * y_hy + gd_ref[0].astype(F32) * y_da
    y = _dot(mix.astype(BF16), wo_ref[...])
    o_ref[0] = s_ref[0] + mod_ref[0, mrow:mrow + 1, :] * _rms(y, g_ref[grow:grow + 1, :])


def _merge(s, yh, od, p, mod, g, w_hy_out, w_da_out, w_o, *, mrow, grow, gate_chunk0, tm):
    b, t, d = s.shape
    tok = lambda c: pl.BlockSpec((1, tm, d), lambda bi, i: (bi, i, c))
    wfull = lambda w: pl.BlockSpec(w.shape, lambda bi, i: (0, 0))
    return pl.pallas_call(
        functools.partial(_merge_kernel, mrow, grow),
        grid=(b, t // tm),
        in_specs=[tok(0), tok(0), tok(0), tok(gate_chunk0), tok(gate_chunk0 + 1),
                  pl.BlockSpec((1, N_MOD, d), lambda bi, i: (bi, 0, 0)),
                  pl.BlockSpec(g.shape, lambda bi, i: (0, 0)),
                  wfull(w_hy_out), wfull(w_da_out), wfull(w_o)],
        out_specs=tok(0),
        out_shape=jax.ShapeDtypeStruct((b, t, d), F32),
        compiler_params=_params(("parallel", "parallel")),
        name="merge",
    )(s, yh, od, p, p, mod, g, w_hy_out, w_da_out, w_o)


def kernel(x, c, ctx, c_ctx, w_ada, b_ada, norm_g, w_ff_in, w_ff_out, w_in, hy_conv_w, hy_conv_b, filt_w1, filt_b1, filt_w2, filt_b2, filt_w3, filt_b3, filt_w4, filt_freq, hy_bias, lambda_q1, lambda_k1, lambda_q2, lambda_k2, subln_g, w_hy_out, w_da_out, w_o):
    b, seq, d = x.shape
    lctx = ctx.shape[1]
    depth = w_ada.shape[0]
    d_hy = hy_bias.shape[1]
    assert depth == 1 and b % 2 == 0 and b < 16
    ctx_row = b
    tm = min(512, seq)
    tq = min(256, seq)
    ct = LANES

    rope = _rope_tables(seq)
    f1_data, f1_real, f1_inv, dft_g, dft_h = _dft_tables(seq)
    c16 = jnp.zeros((16, d), F32).at[:b].set(c).at[ctx_row].set(c_ctx)
    latent_row = lambda bi: bi
    context_row = lambda bi: ctx_row
    modes = ("plain",) * (3 * d_hy // d) + ("rope_q", "rope_k", "plain", "sigmoid", "sigmoid")
    kv_chunk0 = 3 * d_hy // d + 1

    xs, cs = x, ctx
    for l in range(depth):
        lam_init = 0.8 - 0.6 * math.exp(-0.3 * l)
        mod = _ada(c16, w_ada[l], b_ada[l]).reshape(16, N_MOD, d)
        g = norm_g[l]
        wfi = w_ff_in[l].astype(BF16)
        wfo = w_ff_out[l].astype(BF16)
        wi = w_in[l].astype(BF16)
        filt = (filt_w1[l], filt_b1[l], filt_w2[l], filt_b2[l], filt_w3[l], filt_b3[l],
                filt_w4[l], filt_freq[l])
        lam_rows = jnp.stack([lambda_q1[l], lambda_k1[l], lambda_q2[l], lambda_k2[l]])

        xs = _ffn(xs, mod, g, wfi[0], wfo[0], mrow=0, grow=0, mod_row_of_batch=latent_row, tm=tm)
        cs = _ffn(cs, mod, g, wfi[0], wfo[0], mrow=0, grow=0, mod_row_of_batch=context_row,
                  tm=min(tm, lctx))

        p = _proj(xs, mod, g, wi, modes=modes, col_chunk0=0, mrow=3, grow=2,
                  mod_row_of_batch=latent_row, tm=tm, rope=rope)
        pc = _proj(cs, mod, g, wi, modes=("plain", "plain"), col_chunk0=kv_chunk0, mrow=3, grow=2,
                   mod_row_of_batch=context_row, tm=min(tm, lctx))

        kf = _hyena_filter_spectrum(seq, d_hy, filt, (f1_real, dft_g), ct)
        yh = _hyena(p, hy_conv_w[l], hy_conv_b[l], hy_bias[l], kf, (f1_data, f1_inv, dft_g, dft_h), d_hy, ct)
        od = _attn(p, pc, lam_rows, subln_g[l], lam_init, d_hy, tq)
        xs = _merge(xs, yh, od, p, mod, g, w_hy_out[l].astype(BF16), w_da_out[l].astype(BF16),
                    w_o[l].astype(BF16), mrow=5, grow=3, gate_chunk0=kv_chunk0 + 2, tm=tm)
        xs = _ffn(xs, mod, g, wfi[1], wfo[1], mrow=6, grow=4, mod_row_of_batch=latent_row, tm=tm)
    return xs
```

```python
import functools
import math

import jax
import jax.numpy as jnp
from jax import lax
from jax.experimental import pallas as pl
from jax.experimental.pallas import tpu as pltpu

F32 = jnp.float32
BF16 = jnp.bfloat16
HIGHEST = lax.Precision.HIGHEST

N_HEADS = 8
HEAD_DIM = 64
GRID_W = 64
ROPE_THETA = 10000.0
SHORT_CONV = 3
FILTER_EMB = 33
DECAY_TARGET = 1e-2
FAST_DECAY_PCT = 0.3
SLOW_DECAY_PCT = 1.5
EPS = 1e-6
N_MOD = 9
Q_SCALE = HEAD_DIM ** -0.5 * math.log2(math.e)

LANES = 128
SUBLANES = 8
MXU_DIM = 256
DFT_N2 = 128
VMEM_LIMIT = 56 * 1024 * 1024


def _params(sem, vmem=VMEM_LIMIT):
    return pltpu.CompilerParams(dimension_semantics=sem, vmem_limit_bytes=vmem)


def _rms(x, g):
    return x * lax.rsqrt(jnp.mean(x * x, axis=-1, keepdims=True) + EPS) * g


def _dot(a, b, **kw):
    return jnp.dot(a, b, preferred_element_type=F32, **kw)


def _norm_modulate(x, mod_ref, g_ref, mrow, grow):
    shift = mod_ref[0, mrow:mrow + 1, :]
    scale = mod_ref[0, mrow + 1:mrow + 2, :]
    return (_rms(x, g_ref[grow:grow + 1, :]) * (1.0 + scale) + shift).astype(BF16)


def _ada_kernel(c_ref, w_ref, b_ref, o_ref):
    c = c_ref[...]
    a = c * jax.nn.sigmoid(c)
    o_ref[...] = _dot(a, w_ref[...], precision=HIGHEST) + b_ref[...]


def _ada(c16, w_ada, b_ada):
    d = c16.shape[1]
    n = w_ada.shape[1]
    return pl.pallas_call(
        _ada_kernel,
        grid=(n // d,),
        in_specs=[pl.BlockSpec((16, d), lambda j: (0, 0)),
                  pl.BlockSpec((d, d), lambda j: (0, j)),
                  pl.BlockSpec((1, d), lambda j: (0, j))],
        out_specs=pl.BlockSpec((16, d), lambda j: (0, j)),
        out_shape=jax.ShapeDtypeStruct((16, n), F32),
        compiler_params=_params(("arbitrary",)),
        name="ada",
    )(c16, w_ada, b_ada.reshape(1, n))


def _ffn_kernel(mrow, grow, bounds, s_ref, mod_ref, g_ref, wi_ref, wo_ref, o_ref):
    dff = wo_ref.shape[0]
    x = s_ref[0]
    xn = _norm_modulate(x, mod_ref, g_ref, mrow, grow)
    acc = None
    for lo, hi in zip(bounds[:-1], bounds[1:]):
        hg = _dot(xn, wi_ref[:, lo:hi])
        hu = _dot(xn, wi_ref[:, dff + lo:dff + hi])
        h = (hg * jax.nn.sigmoid(hg) * hu).astype(BF16)
        part = _dot(h, wo_ref[lo:hi, :])
        acc = part if acc is None else acc + part
    gate = mod_ref[0, mrow + 2:mrow + 3, :]
    o_ref[0] = x + 0.5 * gate * _rms(acc, g_ref[grow + 1:grow + 2, :])


def _ffn(s, mod, g, w_in, w_out, *, mrow, grow, mod_row_of_batch, tm):
    b, t, d = s.shape
    dff = w_out.shape[0]
    half = (dff // MXU_DIM + 1) // 2 * MXU_DIM
    bounds = (0, half, dff)
    once = pl.Buffered(1)
    return pl.pallas_call(
        functools.partial(_ffn_kernel, mrow, grow, bounds),
        grid=(b, t // tm),
        in_specs=[pl.BlockSpec((1, tm, d), lambda bi, i: (bi, i, 0)),
                  pl.BlockSpec((1, N_MOD, d), lambda bi, i: (mod_row_of_batch(bi), 0, 0)),
                  pl.BlockSpec(g.shape, lambda bi, i: (0, 0)),
                  pl.BlockSpec(w_in.shape, lambda bi, i: (0, 0), pipeline_mode=once),
                  pl.BlockSpec(w_out.shape, lambda bi, i: (0, 0), pipeline_mode=once)],
        out_specs=pl.BlockSpec((1, tm, d), lambda bi, i: (bi, i, 0)),
        out_shape=jax.ShapeDtypeStruct((b, t, d), F32),
        compiler_params=_params(("parallel", "parallel")),
        name="ffn",
    )(s, mod, g, w_in, w_out)


def _rope(acc, cos, sa, sb):
    cols = []
    for k in range(acc.shape[1] // LANES):
        blk = acc[:, k * LANES:(k + 1) * LANES]
        cols.append(blk * cos + pltpu.roll(blk, 16, 1) * sa + pltpu.roll(blk, LANES - 16, 1) * sb)
    return jnp.concatenate(cols, axis=1)


def _proj_kernel(modes, mrow, grow, x_ref, mod_ref, g_ref, w_ref, *rest):
    o_ref = rest[-1]
    d = x_ref.shape[2]
    xn = _norm_modulate(x_ref[0], mod_ref, g_ref, mrow, grow)
    for jj, mode in enumerate(modes):
        acc = _dot(xn, w_ref[:, jj * d:(jj + 1) * d])
        if mode == "sigmoid":
            acc = jax.nn.sigmoid(acc)
        elif mode in ("rope_q", "rope_k"):
            cos_ref, sa_ref, sb_ref = rest[:3]
            acc = _rope(acc, cos_ref[...], sa_ref[...], sb_ref[...])
            if mode == "rope_q":
                acc = acc * Q_SCALE
        o_ref[0, :, jj * d:(jj + 1) * d] = acc.astype(BF16)


def _proj(x, mod, g, w, *, modes, col_chunk0, mrow, grow, mod_row_of_batch, tm, rope=None):
    b, t, d = x.shape
    nch = len(modes)
    assert col_chunk0 % nch == 0
    in_specs = [pl.BlockSpec((1, tm, d), lambda bi, i: (bi, i, 0)),
                pl.BlockSpec((1, N_MOD, d), lambda bi, i: (mod_row_of_batch(bi), 0, 0)),
                pl.BlockSpec(g.shape, lambda bi, i: (0, 0)),
                pl.BlockSpec((d, nch * d), lambda bi, i: (0, col_chunk0 // nch),
                             pipeline_mode=pl.Buffered(1))]
    args = [x, mod, g, w]
    if rope is not None:
        in_specs += [pl.BlockSpec((tm, LANES), lambda bi, i: (i, 0))] * 3
        args += list(rope)
    return pl.pallas_call(
        functools.partial(_proj_kernel, modes, mrow, grow),
        grid=(b, t // tm),
        in_specs=in_specs,
        out_specs=pl.BlockSpec((1, tm, nch * d), lambda bi, i: (bi, i, 0)),
        out_shape=jax.ShapeDtypeStruct((b, t, nch * d), BF16),
        compiler_params=_params(("parallel", "parallel")),
        name="proj",
    )(*args)


def _rope_tables(seq):
    pos = jnp.arange(seq)
    r = (pos // GRID_W).astype(F32)[:, None]
    col = (pos % GRID_W).astype(F32)[:, None]
    half = HEAD_DIM // 2
    inv = ROPE_THETA ** (-jnp.arange(0, half, 2, dtype=F32) / half)
    cr, sr = jnp.cos(r * inv), jnp.sin(r * inv)
    cc, sc = jnp.cos(col * inv), jnp.sin(col * inv)
    z = jnp.zeros_like(sr)
    cos = jnp.concatenate([cr, cr, cc, cc], axis=1)
    sa = jnp.concatenate([z, sr, z, sc], axis=1)
    sb = jnp.concatenate([-sr, z, -sc, z], axis=1)
    rep = LANES // HEAD_DIM
    return tuple(jnp.tile(t, (1, rep)) for t in (cos, sa, sb))


def _dft_tables(seq):
    n = 2 * seq
    n2 = DFT_N2
    n1 = n // n2
    k1h = n1 // 2

    def cs(num, den):
        ang = (2.0 * math.pi / den) * (num % den).astype(F32)
        return jnp.cos(ang), jnp.sin(ang)

    i1 = jnp.arange(n1)
    c, s = cs(i1[:, None] * i1[None, :], n1)
    f1_data = jnp.concatenate([jnp.concatenate([c[:, :k1h], s[:, :k1h]], 1),
                               jnp.concatenate([-s[:, :k1h], c[:, :k1h]], 1)], 0)
    f1_real = jnp.concatenate([c, -s], 0)
    ci, si = c[:k1h, :] / n, s[:k1h, :] / n
    f1_inv = jnp.concatenate([jnp.concatenate([ci, -si], 1),
                              jnp.concatenate([si, ci], 1)], 0)
    i2 = jnp.arange(n2)
    num = i2[None, None, :] * (i1[:, None, None] + n1 * i2[None, :, None])
    c2, s2 = cs(num, n)
    g = jnp.concatenate([jnp.concatenate([c2, s2], 2),
                         jnp.concatenate([-s2, c2], 2)], 1)
    h = jnp.swapaxes(g, 1, 2)
    return (f1_data.astype(BF16), f1_real.astype(BF16), f1_inv.astype(BF16),
            g.astype(BF16), h.astype(BF16))


def _block_rows(m, size):
    return pl.ds(pl.multiple_of(m * size, size), size)


def _dft_stage1(src_rows, f1, r_ref, n1, group):
    ct = r_ref.shape[1]

    def body(i, carry):
        m0 = i * group
        z = jnp.concatenate([src_rows(m0 + j) for j in range(group)], axis=1).astype(BF16)
        r = _dot(f1, z)
        for j in range(group):
            r_ref[_block_rows(m0 + j, 2 * n1), :] = r[:, j * ct:(j + 1) * ct]
        return carry

    lax.fori_loop(0, DFT_N2 // group, body, 0)


def _stage2_operand(r_ref, q, n1):
    return jnp.concatenate([r_ref[pl.ds(q, DFT_N2, stride=2 * n1), :],
                            r_ref[pl.ds(n1 + q, DFT_N2, stride=2 * n1), :]], axis=0).astype(BF16)


def _filt_mlp_kernel(z_ref, w1_ref, b1_ref, w2_ref, b2_ref, w3_ref, b3_ref, fr_ref, o_ref):
    fr = fr_ref[...]
    h = jnp.sin(fr * (_dot(z_ref[...], w1_ref[...], precision=HIGHEST) + b1_ref[...]))
    h = jnp.sin(fr * (_dot(h, w2_ref[...], precision=HIGHEST) + b2_ref[...]))
    o_ref[...] = jnp.sin(fr * (_dot(h, w3_ref[...], precision=HIGHEST) + b3_ref[...]))


def _filt_spec_kernel(seq, n1, group1, group2, h_ref, w4f_ref, w4b_ref, t_ref, dl_ref, f1_ref, g_ref,
                      o_ref, k_ref, r_ref):
    n2 = DFT_N2
    n = 2 * seq
    ct = o_ref.shape[2]
    h3 = h_ref[...]
    row = lax.broadcasted_iota(jnp.int32, (n, ct), 0)
    hk = jnp.where(row < seq, _dot(h3, w4f_ref[...], precision=HIGHEST),
                   _dot(h3, w4b_ref[...], precision=HIGHEST))
    kraw = jnp.where(row == seq, 0.0, hk * jnp.exp(-t_ref[...] * dl_ref[...]))
    k_ref[...] = kraw / (jnp.sum(jnp.abs(kraw), axis=0, keepdims=True) + EPS)

    _dft_stage1(lambda m: k_ref[pl.ds(m, n1, stride=n2), :], f1_ref[...], r_ref, n1, group1)

    def stage2(i, carry):
        q0 = i * group2
        ops = [_stage2_operand(r_ref, q0 + j, n1) for j in range(group2)]
        for j in range(group2):
            o_ref[q0 + j] = _dot(g_ref[q0 + j], ops[j]).astype(o_ref.dtype)
        return carry

    lax.fori_loop(0, n1 // group2, stage2, 0)


def _hyena_filter_spectrum(seq, d_hy, filt, tables, ct):
    w1, b1, w2, b2, w3, b3, w4, freq = filt
    f1_real, g = tables
    n = 2 * seq
    n2 = DFT_N2
    n1 = n // n2
    order = w1.shape[1]
    pad = LANES - order
    bands = (FILTER_EMB - 1) // 2
    t = jnp.linspace(0.0, 1.0, seq, dtype=F32)[:, None]
    w = 2.0 * math.pi * jnp.arange(seq, dtype=F32)[:, None] / seq
    f = jnp.linspace(1e-4, bands - 1, bands, dtype=F32)[None, :]
    z = jnp.concatenate([t, jnp.cos(f * w), -jnp.sin(f * w)], axis=-1)
    idx = jnp.concatenate([jnp.arange(seq), jnp.zeros((1,), jnp.int32), jnp.arange(seq - 1, 0, -1)])
    z_ext = jnp.pad(z[idx], ((0, 0), (0, LANES - FILTER_EMB)))
    t_ext = t[idx]
    deltas = jnp.abs(jnp.linspace(math.log(DECAY_TARGET) / SLOW_DECAY_PCT,
                                  math.log(DECAY_TARGET) / FAST_DECAY_PCT, d_hy, dtype=F32))[None, :]

    def padc(a):
        return jnp.pad(a, ((0, 0), (0, pad)))

    w1p = jnp.pad(w1, ((0, LANES - FILTER_EMB), (0, pad)))
    w2p = jnp.pad(w2, ((0, pad), (0, pad)))
    w3p = jnp.pad(w3, ((0, pad), (0, pad)))
    w4p = jnp.pad(w4, ((0, pad), (0, 0)))
    h3 = pl.pallas_call(
        _filt_mlp_kernel,
        out_shape=jax.ShapeDtypeStruct((n, LANES), F32),
        compiler_params=pltpu.CompilerParams(vmem_limit_bytes=VMEM_LIMIT),
        name="filt_mlp",
    )(z_ext, w1p, padc(b1[None]), w2p, padc(b2[None]), w3p, padc(b3[None]), padc(freq[None]))

    nct = d_hy // ct
    return pl.pallas_call(
        functools.partial(_filt_spec_kernel, seq, n1, SUBLANES, min(4, n1)),
        grid=(nct,),
        in_specs=[pl.BlockSpec((n, LANES), lambda j: (0, 0)),
                  pl.BlockSpec((LANES, ct), lambda j: (0, j)),
                  pl.BlockSpec((LANES, ct), lambda j: (0, nct + j)),
                  pl.BlockSpec((n, 1), lambda j: (0, 0)),
                  pl.BlockSpec((1, ct), lambda j: (0, j)),
                  pl.BlockSpec(f1_real.shape, lambda j: (0, 0)),
                  pl.BlockSpec(g.shape, lambda j: (0, 0, 0))],
        out_specs=pl.BlockSpec((n1, 2 * n2, ct), lambda j: (0, 0, j)),
        out_shape=jax.ShapeDtypeStruct((n1, 2 * n2, d_hy), BF16),
        scratch_shapes=[pltpu.VMEM((n, ct), F32), pltpu.VMEM((n2 * 2 * n1, ct), F32)],
        compiler_params=_params(("arbitrary",)),
        name="filt_spec",
    )(h3, w4p, w4p, t_ext, deltas, f1_real, g)


def _conv3(z, w, b):
    seq = z.shape[0]
    row = lax.broadcasted_iota(jnp.int32, z.shape, 0)
    zm = jnp.where(row == 0, 0.0, pltpu.roll(z, 1, 0))
    zp = jnp.where(row == seq - 1, 0.0, pltpu.roll(z, seq - 1, 0))
    return zm * w[0:1, :] + z * w[1:2, :] + zp * w[2:3, :] + b


def _hyena_kernel(seq, n1, group1, group2, x0_ref, x1_ref, v_ref, cw0_ref, cw1_ref, cwv_ref,
                  cb0_ref, cb1_ref, cbv_ref, bias_ref, f1_ref, f1i_ref, g_ref, h_ref, kf_ref,
                  o_ref, u_ref, r_ref):
    n2 = DFT_N2
    k1h = n1 // 2
    ct = o_ref.shape[2]

    for bb in range(2):
        x1 = _conv3(x1_ref[bb].astype(F32), cw1_ref[...], cb1_ref[...])
        v = _conv3(v_ref[bb].astype(F32), cwv_ref[...], cbv_ref[...])
        u_ref[bb * seq:(bb + 1) * seq, :] = v * x1

    def src_rows(m):
        return jnp.concatenate([u_ref[pl.ds(m, k1h, stride=n2), :],
                                u_ref[pl.ds(seq + m, k1h, stride=n2), :]], axis=0)

    _dft_stage1(src_rows, f1_ref[...], r_ref, n1, group1)

    def stage2(i, carry):
        q0 = i * group2
        ops = [_stage2_operand(r_ref, q0 + j, n1) for j in range(group2)]
        outs = []
        for j in range(group2):
            x = _dot(g_ref[q0 + j], ops[j])
            kf = kf_ref[q0 + j].astype(F32)
            xr, xi, kr, ki = x[:n2], x[n2:], kf[:n2], kf[n2:]
            y = jnp.concatenate([xr * kr - xi * ki, xr * ki + xi * kr], axis=0).astype(BF16)
            outs.append(_dot(h_ref[q0 + j], y))
        for j in range(group2):
            r_ref[pl.ds(q0 + j, n2, stride=2 * n1), :] = outs[j][:n2]
            r_ref[pl.ds(n1 + q0 + j, n2, stride=2 * n1), :] = outs[j][n2:]
        return carry

    lax.fori_loop(0, n1 // group2, stage2, 0)

    f1i = f1i_ref[...]
    bias = bias_ref[...]

    def stage3(i, carry):
        m0 = i * group1
        b = jnp.concatenate([r_ref[_block_rows(m0 + j, 2 * n1), :] for j in range(group1)],
                            axis=1).astype(BF16)
        y = _dot(f1i, b)
        for j in range(group1):
            for bb in range(2):
                rows = pl.ds(bb * seq + m0 + j, k1h, stride=n2)
                u_ref[rows, :] = y[bb * k1h:(bb + 1) * k1h, j * ct:(j + 1) * ct] + u_ref[rows, :] * bias
        return carry

    lax.fori_loop(0, n2 // group1, stage3, 0)

    for bb in range(2):
        x0 = _conv3(x0_ref[bb].astype(F32), cw0_ref[...], cb0_ref[...])
        o_ref[bb] = (u_ref[bb * seq:(bb + 1) * seq, :] * x0).astype(BF16)


def _hyena(p, conv_w, conv_b, hy_bias, kf, tables, d_hy, ct):
    b, seq, _ = p.shape
    f1_data, f1_inv, g, h = tables
    n2 = DFT_N2
    n1 = 2 * seq // n2
    nct = d_hy // ct
    once = pl.Buffered(1)
    zspec = lambda off: pl.BlockSpec((2, seq, ct), lambda j, q: (q, 0, off * nct + j))
    wspec = lambda off: pl.BlockSpec((SHORT_CONV, ct), lambda j, q: (0, off * nct + j))
    bspec = lambda off: pl.BlockSpec((1, ct), lambda j, q: (0, off * nct + j))
    cb = conv_b.reshape(1, -1)
    return pl.pallas_call(
        functools.partial(_hyena_kernel, seq, n1, SUBLANES, min(4, n1)),
        grid=(nct, b // 2),
        in_specs=[zspec(0), zspec(1), zspec(2), wspec(0), wspec(1), wspec(2), bspec(0), bspec(1), bspec(2),
                  pl.BlockSpec((1, ct), lambda j, q: (0, j)),
                  pl.BlockSpec(f1_data.shape, lambda j, q: (0, 0)),
                  pl.BlockSpec(f1_inv.shape, lambda j, q: (0, 0)),
                  pl.BlockSpec(g.shape, lambda j, q: (0, 0, 0), pipeline_mode=once),
                  pl.BlockSpec(h.shape, lambda j, q: (0, 0, 0), pipeline_mode=once),
                  pl.BlockSpec((n1, 2 * n2, ct), lambda j, q: (0, 0, j), pipeline_mode=once)],
        out_specs=pl.BlockSpec((2, seq, ct), lambda j, q: (q, 0, j)),
        out_shape=jax.ShapeDtypeStruct((b, seq, d_hy), BF16),
        scratch_shapes=[pltpu.VMEM((2 * seq, ct), F32), pltpu.VMEM((n2 * 2 * n1, ct), F32)],
        compiler_params=_params(("arbitrary", "arbitrary")),
        name="hyena",
    )(p, p, p, conv_w, conv_w, conv_w, cb, cb, cb, hy_bias.reshape(1, -1), f1_data, f1_inv, g, h, kf)


def _attn_kernel(lam_init, chunks, q_ref, k_ref, v_ref, kc_ref, vc_ref, lam_ref, g_ref, o_ref, vx_ref):
    seq = k_ref.shape[1]
    lctx = kc_ref.shape[1]
    tq, hw = q_ref.shape[1], q_ref.shape[2]

    @pl.when(pl.program_id(2) == 0)
    def _():
        vx_ref[0:seq, 0:hw] = v_ref[0]
        vx_ref[seq:seq + lctx, 0:hw] = vc_ref[0]
        vx_ref[:, hw:2 * hw] = jnp.ones((seq + lctx, hw), BF16)

    q = q_ref[0]
    lane = lax.broadcasted_iota(jnp.int32, q.shape, 1)
    zero = jnp.zeros_like(q)
    qq = jnp.concatenate([jnp.where(lane < HEAD_DIM, q, zero), jnp.where(lane >= HEAD_DIM, q, zero)], axis=0)
    nt = (((1,), (1,)), ((), ()))

    def scores(lo, hi):
        parts = []
        if lo < seq:
            parts.append(lax.dot_general(qq, k_ref[0, lo:min(hi, seq), :], nt, preferred_element_type=F32))
        if hi > seq:
            parts.append(lax.dot_general(qq, kc_ref[0, max(lo, seq) - seq:hi - seq, :], nt,
                                         preferred_element_type=F32))
        return parts[0] if len(parts) == 1 else jnp.concatenate(parts, axis=1)

    m = acc = None
    s_next = scores(*chunks[0])
    for ci, (lo, hi) in enumerate(chunks):
        s = s_next
        if ci + 1 < len(chunks):
            s_next = scores(*chunks[ci + 1])
        mc = jnp.max(s, axis=-1, keepdims=True)
        m_new = mc if m is None else jnp.maximum(m, mc)
        pv = _dot(jnp.exp2(s - m_new).astype(BF16), vx_ref[lo:hi, :])
        acc = pv if acc is None else acc * jnp.exp2(m - m_new) + pv
        m = m_new

    o12 = acc[:, :hw] / acc[:, hw:]
    lp = lam_ref[...]
    lam = (jnp.exp(jnp.sum(lp[0:1] * lp[1:2], axis=-1, keepdims=True))
           - jnp.exp(jnp.sum(lp[2:3] * lp[3:4], axis=-1, keepdims=True)) + lam_init)
    o = o12[:tq] - lam * o12[tq:]
    o_ref[0] = (_rms(o, g_ref[...]) * (1.0 - lam_init)).astype(BF16)


def _attn(p, pc, lam_rows, subln_g, lam_init, d_hy, tq, tk):
    b, seq, _ = p.shape
    lctx = pc.shape[1]
    hw = 2 * HEAD_DIM
    c0 = 3 * d_hy // hw
    nh = N_HEADS
    edges = list(range(0, seq, tk)) + [seq + lctx]
    chunks = tuple(zip(edges[:-1], edges[1:]))
    return pl.pallas_call(
        functools.partial(_attn_kernel, lam_init, chunks),
        grid=(b, nh, seq // tq),
        in_specs=[pl.BlockSpec((1, tq, hw), lambda bi, hh, i: (bi, i, c0 + hh)),
                  pl.BlockSpec((1, seq, hw), lambda bi, hh, i: (bi, 0, c0 + nh + hh)),
                  pl.BlockSpec((1, seq, hw), lambda bi, hh, i: (bi, 0, c0 + 2 * nh + hh)),
                  pl.BlockSpec((1, lctx, hw), lambda bi, hh, i: (bi, 0, hh)),
                  pl.BlockSpec((1, lctx, hw), lambda bi, hh, i: (bi, 0, nh + hh)),
                  pl.BlockSpec(lam_rows.shape, lambda bi, hh, i: (0, 0)),
                  pl.BlockSpec((1, hw), lambda bi, hh, i: (0, 0))],
        out_specs=pl.BlockSpec((1, tq, hw), lambda bi, hh, i: (bi, i, hh)),
        out_shape=jax.ShapeDtypeStruct((b, seq, nh * hw), BF16),
        scratch_shapes=[pltpu.VMEM((seq + lctx, 2 * hw), BF16)],
        compiler_params=_params(("parallel", "parallel", "arbitrary")),
        name="attn",
    )(p, p, p, pc, pc, lam_rows, subln_g.reshape(1, hw))


def _merge_kernel(mrow, grow, s_ref, yh_ref, od_ref, gh_ref, gd_ref, mod_ref, g_ref,
                  why_ref, wda_ref, wo_ref, o_ref):
    y_hy = _dot(yh_ref[0], why_ref[...])
    y_da = _dot(od_ref[0], wda_ref[...])
    mix = gh_ref[0].astype(F32) * y_hy + gd_ref[0].astype(F32) * y_da
    y = _dot(mix.astype(BF16), wo_ref[...])
    o_ref[0] = s_ref[0] + mod_ref[0, mrow:mrow + 1, :] * _rms(y, g_ref[grow:grow + 1, :])


def _merge(s, yh, od, p, mod, g, w_hy_out, w_da_out, w_o, *, mrow, grow, gate_chunk0, tm):
    b, t, d = s.shape
    tok = lambda c: pl.BlockSpec((1, tm, d), lambda bi, i: (bi, i, c))
    wfull = lambda w: pl.BlockSpec(w.shape, lambda bi, i: (0, 0))
    return pl.pallas_call(
        functools.partial(_merge_kernel, mrow, grow),
        grid=(b, t // tm),
        in_specs=[tok(0), tok(0), tok(0), tok(gate_chunk0), tok(gate_chunk0 + 1),
                  pl.BlockSpec((1, N_MOD, d), lambda bi, i: (bi, 0, 0)),
                  pl.BlockSpec(g.shape, lambda bi, i: (0, 0)),
                  wfull(w_hy_out), wfull(w_da_out), wfull(w_o)],
        out_specs=tok(0),
        out_shape=jax.ShapeDtypeStruct((b, t, d), F32),
        compiler_params=_params(("parallel", "parallel")),
        name="merge",
    )(s, yh, od, p, p, mod, g, w_hy_out, w_da_out, w_o)


def kernel(x, c, ctx, c_ctx, w_ada, b_ada, norm_g, w_ff_in, w_ff_out, w_in, hy_conv_w, hy_conv_b, filt_w1, filt_b1, filt_w2, filt_b2, filt_w3, filt_b3, filt_w4, filt_freq, hy_bias, lambda_q1, lambda_k1, lambda_q2, lambda_k2, subln_g, w_hy_out, w_da_out, w_o):
    b, seq, d = x.shape
    lctx = ctx.shape[1]
    depth = w_ada.shape[0]
    d_hy = hy_bias.shape[1]
    assert depth == 1 and b % 2 == 0 and b < 16
    ctx_row = b
    tm = min(512, seq)
    tq = min(256, seq)
    tk = min(1024, seq)
    ct = LANES

    rope = _rope_tables(seq)
    f1_data, f1_real, f1_inv, dft_g, dft_h = _dft_tables(seq)
    c16 = jnp.zeros((16, d), F32).at[:b].set(c).at[ctx_row].set(c_ctx)
    latent_row = lambda bi: bi
    context_row = lambda bi: ctx_row
    modes = ("plain",) * (3 * d_hy // d) + ("rope_q", "rope_k", "plain", "sigmoid", "sigmoid")
    kv_chunk0 = 3 * d_hy // d + 1

    xs, cs = x, ctx
    for l in range(depth):
        lam_init = 0.8 - 0.6 * math.exp(-0.3 * l)
        mod = _ada(c16, w_ada[l], b_ada[l]).reshape(16, N_MOD, d)
        g = norm_g[l]
        wfi = w_ff_in[l].astype(BF16)
        wfo = w_ff_out[l].astype(BF16)
        wi = w_in[l].astype(BF16)
        filt = (filt_w1[l], filt_b1[l], filt_w2[l], filt_b2[l], filt_w3[l], filt_b3[l],
                filt_w4[l], filt_freq[l])
        lam_rows = jnp.stack([lambda_q1[l], lambda_k1[l], lambda_q2[l], lambda_k2[l]])

        xs = _ffn(xs, mod, g, wfi[0], wfo[0], mrow=0, grow=0, mod_row_of_batch=latent_row, tm=tm)
        cs = _ffn(cs, mod, g, wfi[0], wfo[0], mrow=0, grow=0, mod_row_of_batch=context_row,
                  tm=min(tm, lctx))

        p = _proj(xs, mod, g, wi, modes=modes, col_chunk0=0, mrow=3, grow=2,
                  mod_row_of_batch=latent_row, tm=tm, rope=rope)
        pc = _proj(cs, mod, g, wi, modes=("plain", "plain"), col_chunk0=kv_chunk0, mrow=3, grow=2,
                   mod_row_of_batch=context_row, tm=min(tm, lctx))

        kf = _hyena_filter_spectrum(seq, d_hy, filt, (f1_real, dft_g), ct)
        yh = _hyena(p, hy_conv_w[l], hy_conv_b[l], hy_bias[l], kf, (f1_data, f1_inv, dft_g, dft_h), d_hy, ct)
        od = _attn(p, pc, lam_rows, subln_g[l], lam_init, d_hy, tq, tk)
        xs = _merge(xs, yh, od, p, mod, g, w_hy_out[l].astype(BF16), w_da_out[l].astype(BF16),
                    w_o[l].astype(BF16), mrow=5, grow=3, gate_chunk0=kv_chunk0 + 2, tm=tm)
        xs = _ffn(xs, mod, g, wfi[1], wfo[1], mrow=6, grow=4, mod_row_of_batch=latent_row, tm=tm)
    return xs
```

```python
import functools
import math

import jax
import jax.numpy as jnp
from jax import lax
from jax.experimental import pallas as pl
from jax.experimental.pallas import tpu as pltpu

F32 = jnp.float32
BF16 = jnp.bfloat16
HIGHEST = lax.Precision.HIGHEST

N_HEADS = 8
HEAD_DIM = 64
GRID_W = 64
ROPE_THETA = 10000.0
SHORT_CONV = 3
FILTER_EMB = 33
DECAY_TARGET = 1e-2
FAST_DECAY_PCT = 0.3
SLOW_DECAY_PCT = 1.5
EPS = 1e-6
N_MOD = 9
Q_SCALE = HEAD_DIM ** -0.5 * math.log2(math.e)
STABILISER_LIMIT = 48.0

LANES = 128
SUBLANES = 8
MXU_DIM = 256
DFT_N2 = 128
STAGE2_GROUP = 8
VMEM_LIMIT = 56 * 1024 * 1024


def _params(sem, vmem=VMEM_LIMIT):
    return pltpu.CompilerParams(dimension_semantics=sem, vmem_limit_bytes=vmem)


def _rms(x, g):
    return x * lax.rsqrt(jnp.mean(x * x, axis=-1, keepdims=True) + EPS) * g


def _dot(a, b, **kw):
    return jnp.dot(a, b, preferred_element_type=F32, **kw)


def _norm_modulate(x, mod_ref, g_ref, mrow, grow):
    shift = mod_ref[0, mrow:mrow + 1, :]
    scale = mod_ref[0, mrow + 1:mrow + 2, :]
    return (_rms(x, g_ref[grow:grow + 1, :]) * (1.0 + scale) + shift).astype(BF16)


def _ada_kernel(c_ref, w_ref, b_ref, o_ref):
    c = c_ref[...]
    a = c * jax.nn.sigmoid(c)
    o_ref[...] = _dot(a, w_ref[...], precision=HIGHEST) + b_ref[...]


def _ada(c16, w_ada, b_ada):
    d = c16.shape[1]
    n = w_ada.shape[1]
    return pl.pallas_call(
        _ada_kernel,
        grid=(n // d,),
        in_specs=[pl.BlockSpec((16, d), lambda j: (0, 0)),
                  pl.BlockSpec((d, d), lambda j: (0, j)),
                  pl.BlockSpec((1, d), lambda j: (0, j))],
        out_specs=pl.BlockSpec((16, d), lambda j: (0, j)),
        out_shape=jax.ShapeDtypeStruct((16, n), F32),
        compiler_params=_params(("arbitrary",)),
        name="ada",
    )(c16, w_ada, b_ada.reshape(1, n))


def _ffn_kernel(mrow, grow, bounds, s_ref, mod_ref, g_ref, wi_ref, wo_ref, o_ref):
    dff = wo_ref.shape[0]
    x = s_ref[0]
    xn = _norm_modulate(x, mod_ref, g_ref, mrow, grow)
    acc = None
    for lo, hi in zip(bounds[:-1], bounds[1:]):
        hg = _dot(xn, wi_ref[:, lo:hi])
        hu = _dot(xn, wi_ref[:, dff + lo:dff + hi])
        h = (hg * jax.nn.sigmoid(hg) * hu).astype(BF16)
        part = _dot(h, wo_ref[lo:hi, :])
        acc = part if acc is None else acc + part
    gate = mod_ref[0, mrow + 2:mrow + 3, :]
    o_ref[0] = x + 0.5 * gate * _rms(acc, g_ref[grow + 1:grow + 2, :])


def _ffn(s, mod, g, w_in, w_out, *, mrow, grow, mod_row_of_batch, tm):
    b, t, d = s.shape
    dff = w_out.shape[0]
    half = (dff // MXU_DIM + 1) // 2 * MXU_DIM
    bounds = (0, half, dff)
    once = pl.Buffered(1)
    return pl.pallas_call(
        functools.partial(_ffn_kernel, mrow, grow, bounds),
        grid=(b, t // tm),
        in_specs=[pl.BlockSpec((1, tm, d), lambda bi, i: (bi, i, 0)),
                  pl.BlockSpec((1, N_MOD, d), lambda bi, i: (mod_row_of_batch(bi), 0, 0)),
                  pl.BlockSpec(g.shape, lambda bi, i: (0, 0)),
                  pl.BlockSpec(w_in.shape, lambda bi, i: (0, 0), pipeline_mode=once),
                  pl.BlockSpec(w_out.shape, lambda bi, i: (0, 0), pipeline_mode=once)],
        out_specs=pl.BlockSpec((1, tm, d), lambda bi, i: (bi, i, 0)),
        out_shape=jax.ShapeDtypeStruct((b, t, d), F32),
        compiler_params=_params(("parallel", "parallel")),
        name="ffn",
    )(s, mod, g, w_in, w_out)


def _rope(acc, cos, sa, sb):
    cols = []
    for k in range(acc.shape[1] // LANES):
        blk = acc[:, k * LANES:(k + 1) * LANES]
        cols.append(blk * cos + pltpu.roll(blk, 16, 1) * sa + pltpu.roll(blk, LANES - 16, 1) * sb)
    return jnp.concatenate(cols, axis=1)


def _proj_kernel(modes, mrow, grow, x_ref, mod_ref, g_ref, w_ref, *rest):
    o_ref = rest[-1]
    d = x_ref.shape[2]
    xn = _norm_modulate(x_ref[0], mod_ref, g_ref, mrow, grow)
    for jj, mode in enumerate(modes):
        acc = _dot(xn, w_ref[:, jj * d:(jj + 1) * d])
        if mode == "sigmoid":
            acc = jax.nn.sigmoid(acc)
        elif mode in ("rope_q", "rope_k"):
            cos_ref, sa_ref, sb_ref = rest[:3]
            acc = _rope(acc, cos_ref[...], sa_ref[...], sb_ref[...])
            if mode == "rope_q":
                acc = acc * Q_SCALE
        o_ref[0, :, jj * d:(jj + 1) * d] = acc.astype(BF16)


def _proj(x, mod, g, w, *, modes, col_chunk0, mrow, grow, mod_row_of_batch, tm, rope=None):
    b, t, d = x.shape
    nch = len(modes)
    assert col_chunk0 % nch == 0
    in_specs = [pl.BlockSpec((1, tm, d), lambda bi, i: (bi, i, 0)),
                pl.BlockSpec((1, N_MOD, d), lambda bi, i: (mod_row_of_batch(bi), 0, 0)),
                pl.BlockSpec(g.shape, lambda bi, i: (0, 0)),
                pl.BlockSpec((d, nch * d), lambda bi, i: (0, col_chunk0 // nch),
                             pipeline_mode=pl.Buffered(1))]
    args = [x, mod, g, w]
    if rope is not None:
        in_specs += [pl.BlockSpec((tm, LANES), lambda bi, i: (i, 0))] * 3
        args += list(rope)
    return pl.pallas_call(
        functools.partial(_proj_kernel, modes, mrow, grow),
        grid=(b, t // tm),
        in_specs=in_specs,
        out_specs=pl.BlockSpec((1, tm, nch * d), lambda bi, i: (bi, i, 0)),
        out_shape=jax.ShapeDtypeStruct((b, t, nch * d), BF16),
        compiler_params=_params(("parallel", "parallel")),
        name="proj",
    )(*args)


def _rope_tables(seq):
    pos = jnp.arange(seq)
    r = (pos // GRID_W).astype(F32)[:, None]
    col = (pos % GRID_W).astype(F32)[:, None]
    half = HEAD_DIM // 2
    inv = ROPE_THETA ** (-jnp.arange(0, half, 2, dtype=F32) / half)
    cr, sr = jnp.cos(r * inv), jnp.sin(r * inv)
    cc, sc = jnp.cos(col * inv), jnp.sin(col * inv)
    z = jnp.zeros_like(sr)
    cos = jnp.concatenate([cr, cr, cc, cc], axis=1)
    sa = jnp.concatenate([z, sr, z, sc], axis=1)
    sb = jnp.concatenate([-sr, z, -sc, z], axis=1)
    rep = LANES // HEAD_DIM
    return tuple(jnp.tile(t, (1, rep)) for t in (cos, sa, sb))


def _dft_tables(seq):
    n = 2 * seq
    n2 = DFT_N2
    n1 = n // n2
    k1h = n1 // 2

    def cs(num, den):
        ang = (2.0 * math.pi / den) * (num % den).astype(F32)
        return jnp.cos(ang), jnp.sin(ang)

    i1 = jnp.arange(n1)
    c, s = cs(i1[:, None] * i1[None, :], n1)
    f1_data = jnp.concatenate([jnp.concatenate([c[:, :k1h], s[:, :k1h]], 1),
                               jnp.concatenate([-s[:, :k1h], c[:, :k1h]], 1)], 0)
    f1_real = jnp.concatenate([c, -s], 0)
    ci, si = c[:k1h, :] / n, s[:k1h, :] / n
    f1_inv = jnp.concatenate([jnp.concatenate([ci, -si], 1),
                              jnp.concatenate([si, ci], 1)], 0)
    i2 = jnp.arange(n2)
    num = i2[None, None, :] * (i1[:, None, None] + n1 * i2[None, :, None])
    c2, s2 = cs(num, n)
    g = jnp.concatenate([jnp.concatenate([c2, s2], 2),
                         jnp.concatenate([-s2, c2], 2)], 1)
    h = jnp.swapaxes(g, 1, 2)
    return (f1_data.astype(BF16), f1_real.astype(BF16), f1_inv.astype(BF16),
            g.astype(BF16), h.astype(BF16))


def _block_rows(m, size):
    return pl.ds(pl.multiple_of(m * size, size), size)


def _dft_stage1(src_rows, f1, r_ref, n1, group):
    ct = r_ref.shape[1]

    def body(i, carry):
        m0 = i * group
        z = jnp.concatenate([src_rows(m0 + j) for j in range(group)], axis=1).astype(BF16)
        r = _dot(f1, z)
        for j in range(group):
            r_ref[_block_rows(m0 + j, 2 * n1), :] = r[:, j * ct:(j + 1) * ct]
        return carry

    lax.fori_loop(0, DFT_N2 // group, body, 0)


def _stage2_operand(r_ref, q, n1):
    return jnp.concatenate([r_ref[pl.ds(q, DFT_N2, stride=2 * n1), :],
                            r_ref[pl.ds(n1 + q, DFT_N2, stride=2 * n1), :]], axis=0).astype(BF16)


def _filt_mlp_kernel(z_ref, w1_ref, b1_ref, w2_ref, b2_ref, w3_ref, b3_ref, fr_ref, o_ref):
    fr = fr_ref[...]
    h = jnp.sin(fr * (_dot(z_ref[...], w1_ref[...], precision=HIGHEST) + b1_ref[...]))
    h = jnp.sin(fr * (_dot(h, w2_ref[...], precision=HIGHEST) + b2_ref[...]))
    o_ref[...] = jnp.sin(fr * (_dot(h, w3_ref[...], precision=HIGHEST) + b3_ref[...]))


def _filt_spec_kernel(seq, n1, group1, group2, h_ref, w4f_ref, w4b_ref, t_ref, dl_ref, f1_ref, g_ref,
                      o_ref, k_ref, r_ref):
    n2 = DFT_N2
    n = 2 * seq
    ct = o_ref.shape[2]
    h3 = h_ref[...]
    row = lax.broadcasted_iota(jnp.int32, (n, ct), 0)
    hk = jnp.where(row < seq, _dot(h3, w4f_ref[...], precision=HIGHEST),
                   _dot(h3, w4b_ref[...], precision=HIGHEST))
    kraw = jnp.where(row == seq, 0.0, hk * jnp.exp(-t_ref[...] * dl_ref[...]))
    k_ref[...] = kraw / (jnp.sum(jnp.abs(kraw), axis=0, keepdims=True) + EPS)

    _dft_stage1(lambda m: k_ref[pl.ds(m, n1, stride=n2), :], f1_ref[...], r_ref, n1, group1)

    def stage2(i, carry):
        q0 = i * group2
        ops = [_stage2_operand(r_ref, q0 + j, n1) for j in range(group2)]
        for j in range(group2):
            o_ref[q0 + j] = _dot(g_ref[q0 + j], ops[j]).astype(o_ref.dtype)
        return carry

    lax.fori_loop(0, n1 // group2, stage2, 0)


def _hyena_filter_spectrum(seq, d_hy, filt, tables, ct):
    w1, b1, w2, b2, w3, b3, w4, freq = filt
    f1_real, g = tables
    n = 2 * seq
    n2 = DFT_N2
    n1 = n // n2
    order = w1.shape[1]
    pad = LANES - order
    bands = (FILTER_EMB - 1) // 2
    t = jnp.linspace(0.0, 1.0, seq, dtype=F32)[:, None]
    w = 2.0 * math.pi * jnp.arange(seq, dtype=F32)[:, None] / seq
    f = jnp.linspace(1e-4, bands - 1, bands, dtype=F32)[None, :]
    z = jnp.concatenate([t, jnp.cos(f * w), -jnp.sin(f * w)], axis=-1)
    idx = jnp.concatenate([jnp.arange(seq), jnp.zeros((1,), jnp.int32), jnp.arange(seq - 1, 0, -1)])
    z_ext = jnp.pad(z[idx], ((0, 0), (0, LANES - FILTER_EMB)))
    t_ext = t[idx]
    deltas = jnp.abs(jnp.linspace(math.log(DECAY_TARGET) / SLOW_DECAY_PCT,
                                  math.log(DECAY_TARGET) / FAST_DECAY_PCT, d_hy, dtype=F32))[None, :]

    def padc(a):
        return jnp.pad(a, ((0, 0), (0, pad)))

    w1p = jnp.pad(w1, ((0, LANES - FILTER_EMB), (0, pad)))
    w2p = jnp.pad(w2, ((0, pad), (0, pad)))
    w3p = jnp.pad(w3, ((0, pad), (0, pad)))
    w4p = jnp.pad(w4, ((0, pad), (0, 0)))
    h3 = pl.pallas_call(
        _filt_mlp_kernel,
        out_shape=jax.ShapeDtypeStruct((n, LANES), F32),
        compiler_params=pltpu.CompilerParams(vmem_limit_bytes=VMEM_LIMIT),
        name="filt_mlp",
    )(z_ext, w1p, padc(b1[None]), w2p, padc(b2[None]), w3p, padc(b3[None]), padc(freq[None]))

    nct = d_hy // ct
    return pl.pallas_call(
        functools.partial(_filt_spec_kernel, seq, n1, SUBLANES, min(STAGE2_GROUP, n1)),
        grid=(nct,),
        in_specs=[pl.BlockSpec((n, LANES), lambda j: (0, 0)),
                  pl.BlockSpec((LANES, ct), lambda j: (0, j)),
                  pl.BlockSpec((LANES, ct), lambda j: (0, nct + j)),
                  pl.BlockSpec((n, 1), lambda j: (0, 0)),
                  pl.BlockSpec((1, ct), lambda j: (0, j)),
                  pl.BlockSpec(f1_real.shape, lambda j: (0, 0)),
                  pl.BlockSpec(g.shape, lambda j: (0, 0, 0))],
        out_specs=pl.BlockSpec((n1, 2 * n2, ct), lambda j: (0, 0, j)),
        out_shape=jax.ShapeDtypeStruct((n1, 2 * n2, d_hy), BF16),
        scratch_shapes=[pltpu.VMEM((n, ct), F32), pltpu.VMEM((n2 * 2 * n1, ct), F32)],
        compiler_params=_params(("arbitrary",)),
        name="filt_spec",
    )(h3, w4p, w4p, t_ext, deltas, f1_real, g)


def _conv3(z, w, b):
    seq = z.shape[0]
    row = lax.broadcasted_iota(jnp.int32, z.shape, 0)
    zm = jnp.where(row == 0, 0.0, pltpu.roll(z, 1, 0))
    zp = jnp.where(row == seq - 1, 0.0, pltpu.roll(z, seq - 1, 0))
    return zm * w[0:1, :] + z * w[1:2, :] + zp * w[2:3, :] + b


def _hyena_kernel(seq, n1, group1, group2, x0_ref, x1_ref, v_ref, cw0_ref, cw1_ref, cwv_ref,
                  cb0_ref, cb1_ref, cbv_ref, bias_ref, f1_ref, f1i_ref, g_ref, h_ref, kf_ref,
                  o_ref, u_ref, r_ref):
    n2 = DFT_N2
    k1h = n1 // 2
    ct = o_ref.shape[2]

    for bb in range(2):
        x1 = _conv3(x1_ref[bb].astype(F32), cw1_ref[...], cb1_ref[...])
        v = _conv3(v_ref[bb].astype(F32), cwv_ref[...], cbv_ref[...])
        u_ref[bb * seq:(bb + 1) * seq, :] = v * x1

    def src_rows(m):
        return jnp.concatenate([u_ref[pl.ds(m, k1h, stride=n2), :],
                                u_ref[pl.ds(seq + m, k1h, stride=n2), :]], axis=0)

    _dft_stage1(src_rows, f1_ref[...], r_ref, n1, group1)

    def stage2(i, carry):
        q0 = i * group2
        ops = [_stage2_operand(r_ref, q0 + j, n1) for j in range(group2)]
        outs = []
        for j in range(group2):
            x = _dot(g_ref[q0 + j], ops[j])
            kf = kf_ref[q0 + j].astype(F32)
            xr, xi, kr, ki = x[:n2], x[n2:], kf[:n2], kf[n2:]
            y = jnp.concatenate([xr * kr - xi * ki, xr * ki + xi * kr], axis=0).astype(BF16)
            outs.append(_dot(h_ref[q0 + j], y))
        for j in range(group2):
            r_ref[pl.ds(q0 + j, n2, stride=2 * n1), :] = outs[j][:n2]
            r_ref[pl.ds(n1 + q0 + j, n2, stride=2 * n1), :] = outs[j][n2:]
        return carry

    lax.fori_loop(0, n1 // group2, stage2, 0)

    f1i = f1i_ref[...]
    bias = bias_ref[...]

    def stage3(i, carry):
        m0 = i * group1
        b = jnp.concatenate([r_ref[_block_rows(m0 + j, 2 * n1), :] for j in range(group1)],
                            axis=1).astype(BF16)
        y = _dot(f1i, b)
        for j in range(group1):
            for bb in range(2):
                rows = pl.ds(bb * seq + m0 + j, k1h, stride=n2)
                u_ref[rows, :] = y[bb * k1h:(bb + 1) * k1h, j * ct:(j + 1) * ct] + u_ref[rows, :] * bias
        return carry

    lax.fori_loop(0, n2 // group1, stage3, 0)

    for bb in range(2):
        x0 = _conv3(x0_ref[bb].astype(F32), cw0_ref[...], cb0_ref[...])
        o_ref[bb] = (u_ref[bb * seq:(bb + 1) * seq, :] * x0).astype(BF16)


def _hyena(p, conv_w, conv_b, hy_bias, kf, tables, d_hy, ct):
    b, seq, _ = p.shape
    f1_data, f1_inv, g, h = tables
    n2 = DFT_N2
    n1 = 2 * seq // n2
    nct = d_hy // ct
    once = pl.Buffered(1)
    zspec = lambda off: pl.BlockSpec((2, seq, ct), lambda j, q: (q, 0, off * nct + j))
    wspec = lambda off: pl.BlockSpec((SHORT_CONV, ct), lambda j, q: (0, off * nct + j))
    bspec = lambda off: pl.BlockSpec((1, ct), lambda j, q: (0, off * nct + j))
    cb = conv_b.reshape(1, -1)
    return pl.pallas_call(
        functools.partial(_hyena_kernel, seq, n1, SUBLANES, min(STAGE2_GROUP, n1)),
        grid=(nct, b // 2),
        in_specs=[zspec(0), zspec(1), zspec(2), wspec(0), wspec(1), wspec(2), bspec(0), bspec(1), bspec(2),
                  pl.BlockSpec((1, ct), lambda j, q: (0, j)),
                  pl.BlockSpec(f1_data.shape, lambda j, q: (0, 0)),
                  pl.BlockSpec(f1_inv.shape, lambda j, q: (0, 0)),
                  pl.BlockSpec(g.shape, lambda j, q: (0, 0, 0), pipeline_mode=once),
                  pl.BlockSpec(h.shape, lambda j, q: (0, 0, 0), pipeline_mode=once),
                  pl.BlockSpec((n1, 2 * n2, ct), lambda j, q: (0, 0, j), pipeline_mode=once)],
        out_specs=pl.BlockSpec((2, seq, ct), lambda j, q: (q, 0, j)),
        out_shape=jax.ShapeDtypeStruct((b, seq, d_hy), BF16),
        scratch_shapes=[pltpu.VMEM((2 * seq, ct), F32), pltpu.VMEM((n2 * 2 * n1, ct), F32)],
        compiler_params=_params(("arbitrary", "arbitrary")),
        name="hyena",
    )(p, p, p, conv_w, conv_w, conv_w, cb, cb, cb, hy_bias.reshape(1, -1), f1_data, f1_inv, g, h, kf)


def _attn_kernel(lam_init, chunks, q_ref, k_ref, v_ref, kc_ref, vc_ref, lam_ref, g_ref, o_ref,
                 kx_ref, vx_ref, kn_ref):
    seq = k_ref.shape[1]
    lctx = kc_ref.shape[1]
    tq, hw = q_ref.shape[1], q_ref.shape[2]
    nt = (((1,), (1,)), ((), ()))

    def map_sums(x2):
        lane = lax.broadcasted_iota(jnp.int32, x2.shape, 1)
        return (jnp.sum(jnp.where(lane < HEAD_DIM, x2, 0.0), axis=1, keepdims=True),
                jnp.sum(jnp.where(lane >= HEAD_DIM, x2, 0.0), axis=1, keepdims=True))

    @pl.when(pl.program_id(2) == 0)
    def _():
        rows = seq + lctx
        vx_ref[0:seq, 0:hw] = v_ref[0]
        vx_ref[seq:rows, 0:hw] = vc_ref[0]
        vx_ref[:, hw:2 * hw] = jnp.ones((rows, hw), BF16)
        kx_ref[0:seq, 0:hw] = k_ref[0]
        kx_ref[seq:rows, 0:hw] = kc_ref[0]
        lane = lax.broadcasted_iota(jnp.int32, (rows, hw), 1)
        kx_ref[:, hw:2 * hw] = jnp.where(lane == 0, 1.0, 0.0).astype(BF16)
        a1, a2 = map_sums(jnp.square(k_ref[0].astype(F32)))
        c1, c2 = map_sums(jnp.square(kc_ref[0].astype(F32)))
        kmax1 = jnp.sqrt(jnp.maximum(jnp.max(a1, axis=0, keepdims=True), jnp.max(c1, axis=0, keepdims=True)))
        kmax2 = jnp.sqrt(jnp.maximum(jnp.max(a2, axis=0, keepdims=True), jnp.max(c2, axis=0, keepdims=True)))
        kn_ref[0:1, :] = jnp.broadcast_to(kmax1, (1, hw))
        kn_ref[1:2, :] = jnp.broadcast_to(kmax2, (1, hw))

    q = q_ref[0]
    lane = lax.broadcasted_iota(jnp.int32, q.shape, 1)
    zero = jnp.zeros_like(q)
    qq = jnp.concatenate([jnp.where(lane < HEAD_DIM, q, zero), jnp.where(lane >= HEAD_DIM, q, zero)], axis=0)
    lane2 = lax.broadcasted_iota(jnp.int32, qq.shape, 1)
    qn1, qn2 = map_sums(jnp.square(q.astype(F32)))
    bound = jnp.concatenate([jnp.sqrt(qn1) * kn_ref[0:1, 0:1], jnp.sqrt(qn2) * kn_ref[1:2, 0:1]], axis=0)
    bounded = jnp.max(bound) <= STABILISER_LIMIT

    def finish(acc):
        o12 = acc[:, :hw] / acc[:, hw:]
        lp = lam_ref[...]
        lam = (jnp.exp(jnp.sum(lp[0:1] * lp[1:2], axis=-1, keepdims=True))
               - jnp.exp(jnp.sum(lp[2:3] * lp[3:4], axis=-1, keepdims=True)) + lam_init)
        o = o12[:tq] - lam * o12[tq:]
        o_ref[0] = (_rms(o, g_ref[...]) * (1.0 - lam_init)).astype(BF16)

    @pl.when(bounded)
    def _():
        qe = jnp.concatenate([qq, jnp.where(lane2 == 0, -bound, 0.0).astype(BF16)], axis=1)
        acc = None
        for lo, hi in chunks:
            s = lax.dot_general(qe, kx_ref[lo:hi, :], nt, preferred_element_type=F32)
            pv = _dot(jnp.exp2(s).astype(BF16), vx_ref[lo:hi, :])
            acc = pv if acc is None else acc + pv
        finish(acc)

    @pl.when(jnp.logical_not(bounded))
    def _():
        qe = jnp.concatenate([qq, jnp.zeros_like(qq)], axis=1)
        m = acc = None
        for lo, hi in chunks:
            s = lax.dot_general(qe, kx_ref[lo:hi, :], nt, preferred_element_type=F32)
            mc = jnp.max(s, axis=-1, keepdims=True)
            m_new = mc if m is None else jnp.maximum(m, mc)
            pv = _dot(jnp.exp2(s - m_new).astype(BF16), vx_ref[lo:hi, :])
            acc = pv if acc is None else acc * jnp.exp2(m - m_new) + pv
            m = m_new
        finish(acc)


def _attn(p, pc, lam_rows, subln_g, lam_init, d_hy, tq, tk):
    b, seq, _ = p.shape
    lctx = pc.shape[1]
    hw = 2 * HEAD_DIM
    c0 = 3 * d_hy // hw
    nh = N_HEADS
    edges = list(range(0, seq, tk)) + [seq + lctx]
    chunks = tuple(zip(edges[:-1], edges[1:]))
    return pl.pallas_call(
        functools.partial(_attn_kernel, lam_init, chunks),
        grid=(b, nh, seq // tq),
        in_specs=[pl.BlockSpec((1, tq, hw), lambda bi, hh, i: (bi, i, c0 + hh)),
                  pl.BlockSpec((1, seq, hw), lambda bi, hh, i: (bi, 0, c0 + nh + hh)),
                  pl.BlockSpec((1, seq, hw), lambda bi, hh, i: (bi, 0, c0 + 2 * nh + hh)),
                  pl.BlockSpec((1, lctx, hw), lambda bi, hh, i: (bi, 0, hh)),
                  pl.BlockSpec((1, lctx, hw), lambda bi, hh, i: (bi, 0, nh + hh)),
                  pl.BlockSpec(lam_rows.shape, lambda bi, hh, i: (0, 0)),
                  pl.BlockSpec((1, hw), lambda bi, hh, i: (0, 0))],
        out_specs=pl.BlockSpec((1, tq, hw), lambda bi, hh, i: (bi, i, hh)),
        out_shape=jax.ShapeDtypeStruct((b, seq, nh * hw), BF16),
        scratch_shapes=[pltpu.VMEM((seq + lctx, 2 * hw), BF16), pltpu.VMEM((seq + lctx, 2 * hw), BF16),
                        pltpu.VMEM((2, hw), F32)],
        compiler_params=_params(("parallel", "parallel", "arbitrary")),
        name="attn",
    )(p, p, p, pc, pc, lam_rows, subln_g.reshape(1, hw))


def _merge_kernel(mrow, grow, s_ref, yh_ref, od_ref, gh_ref, gd_ref, mod_ref, g_ref,
                  why_ref, wda_ref, wo_ref, o_ref):
    y_hy = _dot(yh_ref[0], why_ref[...])
    y_da = _dot(od_ref[0], wda_ref[...])
    mix = gh_ref[0].astype(F32) * y_hy + gd_ref[0].astype(F32) * y_da
    y = _dot(mix.astype(BF16), wo_ref[...])
    o_ref[0] = s_ref[0] + mod_ref[0, mrow:mrow + 1, :] * _rms(y, g_ref[grow:grow + 1, :])


def _merge(s, yh, od, p, mod, g, w_hy_out, w_da_out, w_o, *, mrow, grow, gate_chunk0, tm):
    b, t, d = s.shape
    tok = lambda c: pl.BlockSpec((1, tm, d), lambda bi, i: (bi, i, c))
    wfull = lambda w: pl.BlockSpec(w.shape, lambda bi, i: (0, 0))
    return pl.pallas_call(
        functools.partial(_merge_kernel, mrow, grow),
        grid=(b, t // tm),
        in_specs=[tok(0), tok(0), tok(0), tok(gate_chunk0), tok(gate_chunk0 + 1),
                  pl.BlockSpec((1, N_MOD, d), lambda bi, i: (bi, 0, 0)),
                  pl.BlockSpec(g.shape, lambda bi, i: (0, 0)),
                  wfull(w_hy_out), wfull(w_da_out), wfull(w_o)],
        out_specs=tok(0),
        out_shape=jax.ShapeDtypeStruct((b, t, d), F32),
        compiler_params=_params(("parallel", "parallel")),
        name="merge",
    )(s, yh, od, p, p, mod, g, w_hy_out, w_da_out, w_o)


def kernel(x, c, ctx, c_ctx, w_ada, b_ada, norm_g, w_ff_in, w_ff_out, w_in, hy_conv_w, hy_conv_b, filt_w1, filt_b1, filt_w2, filt_b2, filt_w3, filt_b3, filt_w4, filt_freq, hy_bias, lambda_q1, lambda_k1, lambda_q2, lambda_k2, subln_g, w_hy_out, w_da_out, w_o):
    b, seq, d = x.shape
    lctx = ctx.shape[1]
    depth = w_ada.shape[0]
    d_hy = hy_bias.shape[1]
    assert depth == 1 and b % 2 == 0 and b < 16
    ctx_row = b
    tm = min(512, seq)
    tq = min(256, seq)
    tk = min(1024, seq)
    ct = LANES

    rope = _rope_tables(seq)
    f1_data, f1_real, f1_inv, dft_g, dft_h = _dft_tables(seq)
    c16 = jnp.zeros((16, d), F32).at[:b].set(c).at[ctx_row].set(c_ctx)
    latent_row = lambda bi: bi
    context_row = lambda bi: ctx_row
    modes = ("plain",) * (3 * d_hy // d) + ("rope_q", "rope_k", "plain", "sigmoid", "sigmoid")
    kv_chunk0 = 3 * d_hy // d + 1

    xs, cs = x, ctx
    for l in range(depth):
        lam_init = 0.8 - 0.6 * math.exp(-0.3 * l)
        mod = _ada(c16, w_ada[l], b_ada[l]).reshape(16, N_MOD, d)
        g = norm_g[l]
        wfi = w_ff_in[l].astype(BF16)
        wfo = w_ff_out[l].astype(BF16)
        wi = w_in[l].astype(BF16)
        filt = (filt_w1[l], filt_b1[l], filt_w2[l], filt_b2[l], filt_w3[l], filt_b3[l],
                filt_w4[l], filt_freq[l])
        lam_rows = jnp.stack([lambda_q1[l], lambda_k1[l], lambda_q2[l], lambda_k2[l]])

        xs = _ffn(xs, mod, g, wfi[0], wfo[0], mrow=0, grow=0, mod_row_of_batch=latent_row, tm=tm)
        cs = _ffn(cs, mod, g, wfi[0], wfo[0], mrow=0, grow=0, mod_row_of_batch=context_row,
                  tm=min(tm, lctx))

        p = _proj(xs, mod, g, wi, modes=modes, col_chunk0=0, mrow=3, grow=2,
                  mod_row_of_batch=latent_row, tm=tm, rope=rope)
        pc = _proj(cs, mod, g, wi, modes=("plain", "plain"), col_chunk0=kv_chunk0, mrow=3, grow=2,
                   mod_row_of_batch=context_row, tm=min(tm, lctx))

        kf = _hyena_filter_spectrum(seq, d_hy, filt, (f1_real, dft_g), ct)
        yh = _hyena(p, hy_conv_w[l], hy_conv_b[l], hy_bias[l], kf, (f1_data, f1_inv, dft_g, dft_h), d_hy, ct)
        od = _attn(p, pc, lam_rows, subln_g[l], lam_init, d_hy, tq, tk)
        xs = _merge(xs, yh, od, p, mod, g, w_hy_out[l].astype(BF16), w_da_out[l].astype(BF16),
                    w_o[l].astype(BF16), mrow=5, grow=3, gate_chunk0=kv_chunk0 + 2, tm=tm)
        xs = _ffn(xs, mod, g, wfi[1], wfo[1], mrow=6, grow=4, mod_row_of_batch=latent_row, tm=tm)
    return xs
```

```python
import functools
import math

import jax
import jax.numpy as jnp
from jax import lax
from jax.experimental import pallas as pl
from jax.experimental.pallas import tpu as pltpu

F32 = jnp.float32
BF16 = jnp.bfloat16
HIGHEST = lax.Precision.HIGHEST

N_HEADS = 8
HEAD_DIM = 64
GRID_W = 64
ROPE_THETA = 10000.0
SHORT_CONV = 3
FILTER_EMB = 33
DECAY_TARGET = 1e-2
FAST_DECAY_PCT = 0.3
SLOW_DECAY_PCT = 1.5
EPS = 1e-6
N_MOD = 9
Q_SCALE = HEAD_DIM ** -0.5 * math.log2(math.e)

LANES = 128
SUBLANES = 8
MXU_DIM = 256
DFT_N2 = 128
STAGE2_GROUP = 8
VMEM_LIMIT = 56 * 1024 * 1024


def _params(sem, vmem=VMEM_LIMIT):
    return pltpu.CompilerParams(dimension_semantics=sem, vmem_limit_bytes=vmem)


def _rms(x, g):
    return x * lax.rsqrt(jnp.mean(x * x, axis=-1, keepdims=True) + EPS) * g


def _dot(a, b, **kw):
    return jnp.dot(a, b, preferred_element_type=F32, **kw)


def _norm_modulate(x, mod_ref, g_ref, mrow, grow):
    shift = mod_ref[0, mrow:mrow + 1, :]
    scale = mod_ref[0, mrow + 1:mrow + 2, :]
    return (_rms(x, g_ref[grow:grow + 1, :]) * (1.0 + scale) + shift).astype(BF16)


def _ada_kernel(c_ref, w_ref, b_ref, o_ref):
    c = c_ref[...]
    a = c * jax.nn.sigmoid(c)
    o_ref[...] = _dot(a, w_ref[...], precision=HIGHEST) + b_ref[...]


def _ada(c16, w_ada, b_ada):
    d = c16.shape[1]
    n = w_ada.shape[1]
    return pl.pallas_call(
        _ada_kernel,
        grid=(n // d,),
        in_specs=[pl.BlockSpec((16, d), lambda j: (0, 0)),
                  pl.BlockSpec((d, d), lambda j: (0, j)),
                  pl.BlockSpec((1, d), lambda j: (0, j))],
        out_specs=pl.BlockSpec((16, d), lambda j: (0, j)),
        out_shape=jax.ShapeDtypeStruct((16, n), F32),
        compiler_params=_params(("arbitrary",)),
        name="ada",
    )(c16, w_ada, b_ada.reshape(1, n))


def _ffn_kernel(mrow, grow, bounds, s_ref, mod_ref, g_ref, wi_ref, wo_ref, o_ref):
    dff = wo_ref.shape[0]
    x = s_ref[0]
    xn = _norm_modulate(x, mod_ref, g_ref, mrow, grow)
    acc = None
    for lo, hi in zip(bounds[:-1], bounds[1:]):
        hg = _dot(xn, wi_ref[:, lo:hi])
        hu = _dot(xn, wi_ref[:, dff + lo:dff + hi])
        h = (hg * jax.nn.sigmoid(hg) * hu).astype(BF16)
        part = _dot(h, wo_ref[lo:hi, :])
        acc = part if acc is None else acc + part
    gate = mod_ref[0, mrow + 2:mrow + 3, :]
    o_ref[0] = x + 0.5 * gate * _rms(acc, g_ref[grow + 1:grow + 2, :])


def _ffn(s, mod, g, w_in, w_out, *, mrow, grow, mod_row_of_batch, tm):
    b, t, d = s.shape
    dff = w_out.shape[0]
    half = (dff // MXU_DIM + 1) // 2 * MXU_DIM
    bounds = (0, half, dff)
    once = pl.Buffered(1)
    return pl.pallas_call(
        functools.partial(_ffn_kernel, mrow, grow, bounds),
        grid=(b, t // tm),
        in_specs=[pl.BlockSpec((1, tm, d), lambda bi, i: (bi, i, 0)),
                  pl.BlockSpec((1, N_MOD, d), lambda bi, i: (mod_row_of_batch(bi), 0, 0)),
                  pl.BlockSpec(g.shape, lambda bi, i: (0, 0)),
                  pl.BlockSpec(w_in.shape, lambda bi, i: (0, 0), pipeline_mode=once),
                  pl.BlockSpec(w_out.shape, lambda bi, i: (0, 0), pipeline_mode=once)],
        out_specs=pl.BlockSpec((1, tm, d), lambda bi, i: (bi, i, 0)),
        out_shape=jax.ShapeDtypeStruct((b, t, d), F32),
        compiler_params=_params(("parallel", "parallel")),
        name="ffn",
    )(s, mod, g, w_in, w_out)


def _rope(acc, cos, sa, sb):
    cols = []
    for k in range(acc.shape[1] // LANES):
        blk = acc[:, k * LANES:(k + 1) * LANES]
        cols.append(blk * cos + pltpu.roll(blk, 16, 1) * sa + pltpu.roll(blk, LANES - 16, 1) * sb)
    return jnp.concatenate(cols, axis=1)


def _proj_kernel(modes, mrow, grow, x_ref, mod_ref, g_ref, w_ref, *rest):
    o_ref = rest[-1]
    d = x_ref.shape[2]
    xn = _norm_modulate(x_ref[0], mod_ref, g_ref, mrow, grow)
    for jj, mode in enumerate(modes):
        acc = _dot(xn, w_ref[:, jj * d:(jj + 1) * d])
        if mode == "sigmoid":
            acc = jax.nn.sigmoid(acc)
        elif mode in ("rope_q", "rope_k"):
            cos_ref, sa_ref, sb_ref = rest[:3]
            acc = _rope(acc, cos_ref[...], sa_ref[...], sb_ref[...])
            if mode == "rope_q":
                acc = acc * Q_SCALE
        o_ref[0, :, jj * d:(jj + 1) * d] = acc.astype(BF16)


def _proj(x, mod, g, w, *, modes, col_chunk0, mrow, grow, mod_row_of_batch, tm, rope=None):
    b, t, d = x.shape
    nch = len(modes)
    assert col_chunk0 % nch == 0
    in_specs = [pl.BlockSpec((1, tm, d), lambda bi, i: (bi, i, 0)),
                pl.BlockSpec((1, N_MOD, d), lambda bi, i: (mod_row_of_batch(bi), 0, 0)),
                pl.BlockSpec(g.shape, lambda bi, i: (0, 0)),
                pl.BlockSpec((d, nch * d), lambda bi, i: (0, col_chunk0 // nch),
                             pipeline_mode=pl.Buffered(1))]
    args = [x, mod, g, w]
    if rope is not None:
        in_specs += [pl.BlockSpec((tm, LANES), lambda bi, i: (i, 0))] * 3
        args += list(rope)
    return pl.pallas_call(
        functools.partial(_proj_kernel, modes, mrow, grow),
        grid=(b, t // tm),
        in_specs=in_specs,
        out_specs=pl.BlockSpec((1, tm, nch * d), lambda bi, i: (bi, i, 0)),
        out_shape=jax.ShapeDtypeStruct((b, t, nch * d), BF16),
        compiler_params=_params(("parallel", "parallel")),
        name="proj",
    )(*args)


def _rope_tables(seq):
    pos = jnp.arange(seq)
    r = (pos // GRID_W).astype(F32)[:, None]
    col = (pos % GRID_W).astype(F32)[:, None]
    half = HEAD_DIM // 2
    inv = ROPE_THETA ** (-jnp.arange(0, half, 2, dtype=F32) / half)
    cr, sr = jnp.cos(r * inv), jnp.sin(r * inv)
    cc, sc = jnp.cos(col * inv), jnp.sin(col * inv)
    z = jnp.zeros_like(sr)
    cos = jnp.concatenate([cr, cr, cc, cc], axis=1)
    sa = jnp.concatenate([z, sr, z, sc], axis=1)
    sb = jnp.concatenate([-sr, z, -sc, z], axis=1)
    rep = LANES // HEAD_DIM
    return tuple(jnp.tile(t, (1, rep)) for t in (cos, sa, sb))


def _dft_tables(seq):
    n = 2 * seq
    n2 = DFT_N2
    n1 = n // n2
    k1h = n1 // 2

    def cs(num, den):
        ang = (2.0 * math.pi / den) * (num % den).astype(F32)
        return jnp.cos(ang), jnp.sin(ang)

    i1 = jnp.arange(n1)
    c, s = cs(i1[:, None] * i1[None, :], n1)
    f1_data = jnp.concatenate([jnp.concatenate([c[:, :k1h], s[:, :k1h]], 1),
                               jnp.concatenate([-s[:, :k1h], c[:, :k1h]], 1)], 0)
    f1_real = jnp.concatenate([c, -s], 0)
    ci, si = c[:k1h, :] / n, s[:k1h, :] / n
    f1_inv = jnp.concatenate([jnp.concatenate([ci, -si], 1),
                              jnp.concatenate([si, ci], 1)], 0)
    i2 = jnp.arange(n2)
    num = i2[None, None, :] * (i1[:, None, None] + n1 * i2[None, :, None])
    c2, s2 = cs(num, n)
    g = jnp.concatenate([jnp.concatenate([c2, s2], 2),
                         jnp.concatenate([-s2, c2], 2)], 1)
    h = jnp.swapaxes(g, 1, 2)
    return (f1_data.astype(BF16), f1_real.astype(BF16), f1_inv.astype(BF16),
            g.astype(BF16), h.astype(BF16))


def _block_rows(m, size):
    return pl.ds(pl.multiple_of(m * size, size), size)


def _dft_stage1(src_rows, f1, r_ref, n1, group):
    ct = r_ref.shape[1]

    def body(i, carry):
        m0 = i * group
        z = jnp.concatenate([src_rows(m0 + j) for j in range(group)], axis=1).astype(BF16)
        r = _dot(f1, z)
        for j in range(group):
            r_ref[_block_rows(m0 + j, 2 * n1), :] = r[:, j * ct:(j + 1) * ct]
        return carry

    lax.fori_loop(0, DFT_N2 // group, body, 0, unroll=2)


def _stage2_operand(r_ref, q, n1):
    return jnp.concatenate([r_ref[pl.ds(q, DFT_N2, stride=2 * n1), :],
                            r_ref[pl.ds(n1 + q, DFT_N2, stride=2 * n1), :]], axis=0).astype(BF16)


def _filt_mlp_kernel(z_ref, w1_ref, b1_ref, w2_ref, b2_ref, w3_ref, b3_ref, fr_ref, o_ref):
    fr = fr_ref[...]
    h = jnp.sin(fr * (_dot(z_ref[...], w1_ref[...], precision=HIGHEST) + b1_ref[...]))
    h = jnp.sin(fr * (_dot(h, w2_ref[...], precision=HIGHEST) + b2_ref[...]))
    o_ref[...] = jnp.sin(fr * (_dot(h, w3_ref[...], precision=HIGHEST) + b3_ref[...]))


def _filt_spec_kernel(seq, n1, group1, group2, h_ref, w4f_ref, w4b_ref, t_ref, dl_ref, f1_ref, g_ref,
                      o_ref, k_ref, r_ref):
    n2 = DFT_N2
    n = 2 * seq
    ct = o_ref.shape[2]
    row = lax.broadcasted_iota(jnp.int32, (n, ct), 0)
    hk = jnp.concatenate([_dot(h_ref[0:seq, :], w4f_ref[...], precision=HIGHEST),
                          _dot(h_ref[seq:n, :], w4b_ref[...], precision=HIGHEST)], axis=0)
    kraw = jnp.where(row == seq, 0.0, hk * jnp.exp(-t_ref[...] * dl_ref[...]))
    k_ref[...] = kraw / (jnp.sum(jnp.abs(kraw), axis=0, keepdims=True) + EPS)

    _dft_stage1(lambda m: k_ref[pl.ds(m, n1, stride=n2), :], f1_ref[...], r_ref, n1, group1)

    def stage2(i, carry):
        q0 = i * group2
        ops = [_stage2_operand(r_ref, q0 + j, n1) for j in range(group2)]
        for j in range(group2):
            o_ref[q0 + j] = _dot(g_ref[q0 + j], ops[j]).astype(o_ref.dtype)
        return carry

    lax.fori_loop(0, n1 // group2, stage2, 0)


def _hyena_filter_spectrum(seq, d_hy, filt, tables, ct):
    w1, b1, w2, b2, w3, b3, w4, freq = filt
    f1_real, g = tables
    n = 2 * seq
    n2 = DFT_N2
    n1 = n // n2
    order = w1.shape[1]
    pad = LANES - order
    bands = (FILTER_EMB - 1) // 2
    t = jnp.linspace(0.0, 1.0, seq, dtype=F32)[:, None]
    w = 2.0 * math.pi * jnp.arange(seq, dtype=F32)[:, None] / seq
    f = jnp.linspace(1e-4, bands - 1, bands, dtype=F32)[None, :]
    z = jnp.concatenate([t, jnp.cos(f * w), -jnp.sin(f * w)], axis=-1)
    def circular(a):
        return jnp.concatenate([a, a[:1], a[:0:-1]], axis=0)

    z_ext = jnp.pad(circular(z), ((0, 0), (0, LANES - FILTER_EMB)))
    t_ext = circular(t)
    deltas = jnp.abs(jnp.linspace(math.log(DECAY_TARGET) / SLOW_DECAY_PCT,
                                  math.log(DECAY_TARGET) / FAST_DECAY_PCT, d_hy, dtype=F32))[None, :]

    def padc(a):
        return jnp.pad(a, ((0, 0), (0, pad)))

    w1p = jnp.pad(w1, ((0, LANES - FILTER_EMB), (0, pad)))
    w2p = jnp.pad(w2, ((0, pad), (0, pad)))
    w3p = jnp.pad(w3, ((0, pad), (0, pad)))
    w4p = jnp.pad(w4, ((0, pad), (0, 0)))
    h3 = pl.pallas_call(
        _filt_mlp_kernel,
        out_shape=jax.ShapeDtypeStruct((n, LANES), F32),
        compiler_params=pltpu.CompilerParams(vmem_limit_bytes=VMEM_LIMIT),
        name="filt_mlp",
    )(z_ext, w1p, padc(b1[None]), w2p, padc(b2[None]), w3p, padc(b3[None]), padc(freq[None]))

    nct = d_hy // ct
    return pl.pallas_call(
        functools.partial(_filt_spec_kernel, seq, n1, SUBLANES, min(STAGE2_GROUP, n1)),
        grid=(nct,),
        in_specs=[pl.BlockSpec((n, LANES), lambda j: (0, 0)),
                  pl.BlockSpec((LANES, ct), lambda j: (0, j)),
                  pl.BlockSpec((LANES, ct), lambda j: (0, nct + j)),
                  pl.BlockSpec((n, 1), lambda j: (0, 0)),
                  pl.BlockSpec((1, ct), lambda j: (0, j)),
                  pl.BlockSpec(f1_real.shape, lambda j: (0, 0)),
                  pl.BlockSpec(g.shape, lambda j: (0, 0, 0))],
        out_specs=pl.BlockSpec((n1, 2 * n2, ct), lambda j: (0, 0, j)),
        out_shape=jax.ShapeDtypeStruct((n1, 2 * n2, d_hy), BF16),
        scratch_shapes=[pltpu.VMEM((n, ct), F32), pltpu.VMEM((n2 * 2 * n1, ct), F32)],
        compiler_params=_params(("arbitrary",)),
        name="filt_spec",
    )(h3, w4p, w4p, t_ext, deltas, f1_real, g)


def _conv3(z, w, b):
    seq = z.shape[0]
    row = lax.broadcasted_iota(jnp.int32, z.shape, 0)
    zm = jnp.where(row == 0, 0.0, pltpu.roll(z, 1, 0))
    zp = jnp.where(row == seq - 1, 0.0, pltpu.roll(z, seq - 1, 0))
    return zm * w[0:1, :] + z * w[1:2, :] + zp * w[2:3, :] + b


def _hyena_kernel(seq, n1, group1, group2, x0_ref, x1_ref, v_ref, cw0_ref, cw1_ref, cwv_ref,
                  cb0_ref, cb1_ref, cbv_ref, bias_ref, f1_ref, f1i_ref, g_ref, h_ref, kf_ref,
                  o_ref, u_ref, r_ref):
    n2 = DFT_N2
    k1h = n1 // 2
    ct = o_ref.shape[2]

    for bb in range(2):
        x1 = _conv3(x1_ref[bb].astype(F32), cw1_ref[...], cb1_ref[...])
        v = _conv3(v_ref[bb].astype(F32), cwv_ref[...], cbv_ref[...])
        u_ref[bb * seq:(bb + 1) * seq, :] = v * x1

    def src_rows(m):
        return jnp.concatenate([u_ref[pl.ds(m, k1h, stride=n2), :],
                                u_ref[pl.ds(seq + m, k1h, stride=n2), :]], axis=0)

    _dft_stage1(src_rows, f1_ref[...], r_ref, n1, group1)

    def stage2(i, carry):
        q0 = i * group2
        ops = [_stage2_operand(r_ref, q0 + j, n1) for j in range(group2)]
        outs = []
        for j in range(group2):
            x = _dot(g_ref[q0 + j], ops[j])
            kf = kf_ref[q0 + j].astype(F32)
            xr, xi, kr, ki = x[:n2], x[n2:], kf[:n2], kf[n2:]
            y = jnp.concatenate([xr * kr - xi * ki, xr * ki + xi * kr], axis=0).astype(BF16)
            outs.append(_dot(h_ref[q0 + j], y))
        for j in range(group2):
            r_ref[pl.ds(q0 + j, n2, stride=2 * n1), :] = outs[j][:n2]
            r_ref[pl.ds(n1 + q0 + j, n2, stride=2 * n1), :] = outs[j][n2:]
        return carry

    lax.fori_loop(0, n1 // group2, stage2, 0)

    f1i = f1i_ref[...]
    bias = bias_ref[...]

    def stage3(i, carry):
        m0 = i * group1
        b = jnp.concatenate([r_ref[_block_rows(m0 + j, 2 * n1), :] for j in range(group1)],
                            axis=1).astype(BF16)
        y = _dot(f1i, b)
        for j in range(group1):
            for bb in range(2):
                rows = pl.ds(bb * seq + m0 + j, k1h, stride=n2)
                u_ref[rows, :] = y[bb * k1h:(bb + 1) * k1h, j * ct:(j + 1) * ct] + u_ref[rows, :] * bias
        return carry

    lax.fori_loop(0, n2 // group1, stage3, 0, unroll=2)

    for bb in range(2):
        x0 = _conv3(x0_ref[bb].astype(F32), cw0_ref[...], cb0_ref[...])
        o_ref[bb] = (u_ref[bb * seq:(bb + 1) * seq, :] * x0).astype(BF16)


def _hyena(p, conv_w, conv_b, hy_bias, kf, tables, d_hy, ct):
    b, seq, _ = p.shape
    f1_data, f1_inv, g, h = tables
    n2 = DFT_N2
    n1 = 2 * seq // n2
    nct = d_hy // ct
    once = pl.Buffered(1)
    zspec = lambda off: pl.BlockSpec((2, seq, ct), lambda j, q: (q, 0, off * nct + j))
    wspec = lambda off: pl.BlockSpec((SHORT_CONV, ct), lambda j, q: (0, off * nct + j))
    bspec = lambda off: pl.BlockSpec((1, ct), lambda j, q: (0, off * nct + j))
    cb = conv_b.reshape(1, -1)
    return pl.pallas_call(
        functools.partial(_hyena_kernel, seq, n1, SUBLANES, min(STAGE2_GROUP, n1)),
        grid=(nct, b // 2),
        in_specs=[zspec(0), zspec(1), zspec(2), wspec(0), wspec(1), wspec(2), bspec(0), bspec(1), bspec(2),
                  pl.BlockSpec((1, ct), lambda j, q: (0, j)),
                  pl.BlockSpec(f1_data.shape, lambda j, q: (0, 0)),
                  pl.BlockSpec(f1_inv.shape, lambda j, q: (0, 0)),
                  pl.BlockSpec(g.shape, lambda j, q: (0, 0, 0), pipeline_mode=once),
                  pl.BlockSpec(h.shape, lambda j, q: (0, 0, 0), pipeline_mode=once),
                  pl.BlockSpec((n1, 2 * n2, ct), lambda j, q: (0, 0, j), pipeline_mode=once)],
        out_specs=pl.BlockSpec((2, seq, ct), lambda j, q: (q, 0, j)),
        out_shape=jax.ShapeDtypeStruct((b, seq, d_hy), BF16),
        scratch_shapes=[pltpu.VMEM((2 * seq, ct), F32), pltpu.VMEM((n2 * 2 * n1, ct), F32)],
        compiler_params=_params(("arbitrary", "arbitrary")),
        name="hyena",
    )(p, p, p, conv_w, conv_w, conv_w, cb, cb, cb, hy_bias.reshape(1, -1), f1_data, f1_inv, g, h, kf)


def _attn_kernel(lam_init, chunks, tq, q_ref, k_ref, v_ref, kc_ref, vc_ref, lam_ref, g_ref, o_ref, vx_ref):
    seq = k_ref.shape[1]
    lctx = kc_ref.shape[1]
    hw = q_ref.shape[2]
    nt = (((1,), (1,)), ((), ()))

    vx_ref[0:seq, 0:hw] = v_ref[0]
    vx_ref[seq:seq + lctx, 0:hw] = vc_ref[0]
    vx_ref[:, hw:2 * hw] = jnp.ones((seq + lctx, hw), BF16)

    lp = lam_ref[...]
    lam = (jnp.exp(jnp.sum(lp[0:1] * lp[1:2], axis=-1, keepdims=True))
           - jnp.exp(jnp.sum(lp[2:3] * lp[3:4], axis=-1, keepdims=True)) + lam_init)

    def q_tile(i, carry):
        rows = pl.ds(pl.multiple_of(i * tq, tq), tq)
        q = q_ref[0, rows, :]
        lane = lax.broadcasted_iota(jnp.int32, q.shape, 1)
        zero = jnp.zeros_like(q)
        qq = jnp.concatenate([jnp.where(lane < HEAD_DIM, q, zero), jnp.where(lane >= HEAD_DIM, q, zero)],
                             axis=0)

        def scores(lo, hi):
            parts = []
            if lo < seq:
                parts.append(lax.dot_general(qq, k_ref[0, lo:min(hi, seq), :], nt, preferred_element_type=F32))
            if hi > seq:
                parts.append(lax.dot_general(qq, kc_ref[0, max(lo, seq) - seq:hi - seq, :], nt,
                                             preferred_element_type=F32))
            return parts[0] if len(parts) == 1 else jnp.concatenate(parts, axis=1)

        m = acc = None
        for lo, hi in chunks:
            s = scores(lo, hi)
            mc = jnp.max(s, axis=-1, keepdims=True)
            m_new = mc if m is None else jnp.maximum(m, mc)
            pv = _dot(jnp.exp2(s - m_new).astype(BF16), vx_ref[lo:hi, :])
            acc = pv if acc is None else acc * jnp.exp2(m - m_new) + pv
            m = m_new

        o12 = acc[:, :hw] / acc[:, hw:]
        o = o12[:tq] - lam * o12[tq:]
        o_ref[0, rows, :] = (_rms(o, g_ref[...]) * (1.0 - lam_init)).astype(BF16)
        return carry

    lax.fori_loop(0, seq // tq, q_tile, 0)


def _attn(p, pc, lam_rows, subln_g, lam_init, d_hy, tq, tk):
    b, seq, _ = p.shape
    lctx = pc.shape[1]
    hw = 2 * HEAD_DIM
    c0 = 3 * d_hy // hw
    nh = N_HEADS
    edges = list(range(0, seq, tk)) + [seq + lctx]
    chunks = tuple(zip(edges[:-1], edges[1:]))
    return pl.pallas_call(
        functools.partial(_attn_kernel, lam_init, chunks, tq),
        grid=(b, nh),
        in_specs=[pl.BlockSpec((1, seq, hw), lambda bi, hh: (bi, 0, c0 + hh)),
                  pl.BlockSpec((1, seq, hw), lambda bi, hh: (bi, 0, c0 + nh + hh)),
                  pl.BlockSpec((1, seq, hw), lambda bi, hh: (bi, 0, c0 + 2 * nh + hh)),
                  pl.BlockSpec((1, lctx, hw), lambda bi, hh: (bi, 0, hh)),
                  pl.BlockSpec((1, lctx, hw), lambda bi, hh: (bi, 0, nh + hh)),
                  pl.BlockSpec(lam_rows.shape, lambda bi, hh: (0, 0)),
                  pl.BlockSpec((1, hw), lambda bi, hh: (0, 0))],
        out_specs=pl.BlockSpec((1, seq, hw), lambda bi, hh: (bi, 0, hh)),
        out_shape=jax.ShapeDtypeStruct((b, seq, nh * hw), BF16),
        scratch_shapes=[pltpu.VMEM((seq + lctx, 2 * hw), BF16)],
        compiler_params=_params(("parallel", "parallel")),
        name="attn",
    )(p, p, p, pc, pc, lam_rows, subln_g.reshape(1, hw))


def _merge_kernel(mrow, grow, s_ref, yh_ref, od_ref, gh_ref, gd_ref, mod_ref, g_ref,
                  why_ref, wda_ref, wo_ref, o_ref):
    y_hy = _dot(yh_ref[0], why_ref[...])
    y_da = _dot(od_ref[0], wda_ref[...])
    mix = gh_ref[0].astype(F32) * y_hy + gd_ref[0].astype(F32) * y_da
    y = _dot(mix.astype(BF16), wo_ref[...])
    o_ref[0] = s_ref[0] + mod_ref[0, mrow:mrow + 1, :] * _rms(y, g_ref[grow:grow + 1, :])


def _merge(s, yh, od, p, mod, g, w_hy_out, w_da_out, w_o, *, mrow, grow, gate_chunk0, tm):
    b, t, d = s.shape
    tok = lambda c: pl.BlockSpec((1, tm, d), lambda bi, i: (bi, i, c))
    wfull = lambda w: pl.BlockSpec(w.shape, lambda bi, i: (0, 0))
    return pl.pallas_call(
        functools.partial(_merge_kernel, mrow, grow),
        grid=(b, t // tm),
        in_specs=[tok(0), tok(0), tok(0), tok(gate_chunk0), tok(gate_chunk0 + 1),
                  pl.BlockSpec((1, N_MOD, d), lambda bi, i: (bi, 0, 0)),
                  pl.BlockSpec(g.shape, lambda bi, i: (0, 0)),
                  wfull(w_hy_out), wfull(w_da_out), wfull(w_o)],
        out_specs=tok(0),
        out_shape=jax.ShapeDtypeStruct((b, t, d), F32),
        compiler_params=_params(("parallel", "parallel")),
        name="merge",
    )(s, yh, od, p, p, mod, g, w_hy_out, w_da_out, w_o)


def kernel(x, c, ctx, c_ctx, w_ada, b_ada, norm_g, w_ff_in, w_ff_out, w_in, hy_conv_w, hy_conv_b, filt_w1, filt_b1, filt_w2, filt_b2, filt_w3, filt_b3, filt_w4, filt_freq, hy_bias, lambda_q1, lambda_k1, lambda_q2, lambda_k2, subln_g, w_hy_out, w_da_out, w_o):
    b, seq, d = x.shape
    lctx = ctx.shape[1]
    depth = w_ada.shape[0]
    d_hy = hy_bias.shape[1]
    assert depth == 1 and b % 2 == 0 and b < 16
    ctx_row = b
    tm = min(512, seq)
    tq = min(256, seq)
    tk = min(1024, seq)
    ct = LANES

    rope = _rope_tables(seq)
    f1_data, f1_real, f1_inv, dft_g, dft_h = _dft_tables(seq)
    c16 = jnp.zeros((16, d), F32).at[:b].set(c).at[ctx_row].set(c_ctx)
    latent_row = lambda bi: bi
    context_row = lambda bi: ctx_row
    modes = ("plain",) * (3 * d_hy // d) + ("rope_q", "rope_k", "plain", "sigmoid", "sigmoid")
    kv_chunk0 = 3 * d_hy // d + 1

    xs, cs = x, ctx
    for l in range(depth):
        lam_init = 0.8 - 0.6 * math.exp(-0.3 * l)
        mod = _ada(c16, w_ada[l], b_ada[l]).reshape(16, N_MOD, d)
        g = norm_g[l]
        wfi = w_ff_in[l].astype(BF16)
        wfo = w_ff_out[l].astype(BF16)
        wi = w_in[l].astype(BF16)
        filt = (filt_w1[l], filt_b1[l], filt_w2[l], filt_b2[l], filt_w3[l], filt_b3[l],
                filt_w4[l], filt_freq[l])
        lam_rows = jnp.stack([lambda_q1[l], lambda_k1[l], lambda_q2[l], lambda_k2[l]])

        xs = _ffn(xs, mod, g, wfi[0], wfo[0], mrow=0, grow=0, mod_row_of_batch=latent_row, tm=tm)
        cs = _ffn(cs, mod, g, wfi[0], wfo[0], mrow=0, grow=0, mod_row_of_batch=context_row,
                  tm=min(tm, lctx))

        p = _proj(xs, mod, g, wi, modes=modes, col_chunk0=0, mrow=3, grow=2,
                  mod_row_of_batch=latent_row, tm=tm, rope=rope)
        pc = _proj(cs, mod, g, wi, modes=("plain", "plain"), col_chunk0=kv_chunk0, mrow=3, grow=2,
                   mod_row_of_batch=context_row, tm=min(tm, lctx))

        kf = _hyena_filter_spectrum(seq, d_hy, filt, (f1_real, dft_g), ct)
        yh = _hyena(p, hy_conv_w[l], hy_conv_b[l], hy_bias[l], kf, (f1_data, f1_inv, dft_g, dft_h), d_hy, ct)
        od = _attn(p, pc, lam_rows, subln_g[l], lam_init, d_hy, tq, tk)
        xs = _merge(xs, yh, od, p, mod, g, w_hy_out[l].astype(BF16), w_da_out[l].astype(BF16),
                    w_o[l].astype(BF16), mrow=5, grow=3, gate_chunk0=kv_chunk0 + 2, tm=tm)
        xs = _ffn(xs, mod, g, wfi[1], wfo[1], mrow=6, grow=4, mod_row_of_batch=latent_row, tm=tm)
    return xs
```

```python
import functools
import math

import jax
import jax.numpy as jnp
from jax import lax
from jax.experimental import pallas as pl
from jax.experimental.pallas import tpu as pltpu

F32 = jnp.float32
BF16 = jnp.bfloat16
HIGHEST = lax.Precision.HIGHEST

N_HEADS = 8
HEAD_DIM = 64
GRID_W = 64
ROPE_THETA = 10000.0
SHORT_CONV = 3
FILTER_EMB = 33
DECAY_TARGET = 1e-2
FAST_DECAY_PCT = 0.3
SLOW_DECAY_PCT = 1.5
EPS = 1e-6
N_MOD = 9
Q_SCALE = HEAD_DIM ** -0.5 * math.log2(math.e)

LANES = 128
SUBLANES = 8
MXU_DIM = 256
DFT_N2 = 128
STAGE2_GROUP = 8
VMEM_LIMIT = 56 * 1024 * 1024


def _params(sem, vmem=VMEM_LIMIT):
    return pltpu.CompilerParams(dimension_semantics=sem, vmem_limit_bytes=vmem)


def _rms(x, g):
    return x * lax.rsqrt(jnp.mean(x * x, axis=-1, keepdims=True) + EPS) * g


def _dot(a, b, **kw):
    return jnp.dot(a, b, preferred_element_type=F32, **kw)


def _norm_modulate(x, mod_ref, g_ref, mrow, grow):
    shift = mod_ref[0, mrow:mrow + 1, :]
    scale = mod_ref[0, mrow + 1:mrow + 2, :]
    return (_rms(x, g_ref[grow:grow + 1, :]) * (1.0 + scale) + shift).astype(BF16)


def _ada_kernel(c_ref, w_ref, b_ref, o_ref):
    c = c_ref[...]
    a = c * jax.nn.sigmoid(c)
    o_ref[...] = _dot(a, w_ref[...], precision=HIGHEST) + b_ref[...]


def _ada(c16, w_ada, b_ada):
    d = c16.shape[1]
    n = w_ada.shape[1]
    return pl.pallas_call(
        _ada_kernel,
        grid=(n // d,),
        in_specs=[pl.BlockSpec((16, d), lambda j: (0, 0)),
                  pl.BlockSpec((d, d), lambda j: (0, j)),
                  pl.BlockSpec((1, d), lambda j: (0, j))],
        out_specs=pl.BlockSpec((16, d), lambda j: (0, j)),
        out_shape=jax.ShapeDtypeStruct((16, n), F32),
        compiler_params=_params(("arbitrary",)),
        name="ada",
    )(c16, w_ada, b_ada.reshape(1, n))


def _ffn_kernel(mrow, grow, bounds, s_ref, mod_ref, g_ref, wi_ref, wo_ref, o_ref):
    dff = wo_ref.shape[0]
    x = s_ref[0]
    xn = _norm_modulate(x, mod_ref, g_ref, mrow, grow)
    acc = None
    for lo, hi in zip(bounds[:-1], bounds[1:]):
        hg = _dot(xn, wi_ref[:, lo:hi])
        hu = _dot(xn, wi_ref[:, dff + lo:dff + hi])
        h = (hg * jax.nn.sigmoid(hg) * hu).astype(BF16)
        part = _dot(h, wo_ref[lo:hi, :])
        acc = part if acc is None else acc + part
    gate = mod_ref[0, mrow + 2:mrow + 3, :]
    o_ref[0] = x + 0.5 * gate * _rms(acc, g_ref[grow + 1:grow + 2, :])


def _ffn(s, mod, g, w_in, w_out, *, mrow, grow, mod_row_of_batch, tm):
    b, t, d = s.shape
    dff = w_out.shape[0]
    half = (dff // MXU_DIM + 1) // 2 * MXU_DIM
    bounds = (0, half, dff)
    once = pl.Buffered(1)
    return pl.pallas_call(
        functools.partial(_ffn_kernel, mrow, grow, bounds),
        grid=(b, t // tm),
        in_specs=[pl.BlockSpec((1, tm, d), lambda bi, i: (bi, i, 0)),
                  pl.BlockSpec((1, N_MOD, d), lambda bi, i: (mod_row_of_batch(bi), 0, 0)),
                  pl.BlockSpec(g.shape, lambda bi, i: (0, 0)),
                  pl.BlockSpec(w_in.shape, lambda bi, i: (0, 0), pipeline_mode=once),
                  pl.BlockSpec(w_out.shape, lambda bi, i: (0, 0), pipeline_mode=once)],
        out_specs=pl.BlockSpec((1, tm, d), lambda bi, i: (bi, i, 0)),
        out_shape=jax.ShapeDtypeStruct((b, t, d), F32),
        compiler_params=_params(("parallel", "parallel")),
        name="ffn",
    )(s, mod, g, w_in, w_out)


def _rope(acc, cos, sa, sb):
    cols = []
    for k in range(acc.shape[1] // LANES):
        blk = acc[:, k * LANES:(k + 1) * LANES]
        cols.append(blk * cos + pltpu.roll(blk, 16, 1) * sa + pltpu.roll(blk, LANES - 16, 1) * sb)
    return jnp.concatenate(cols, axis=1)


def _proj_kernel(modes, mrow, grow, x_ref, mod_ref, g_ref, w_ref, *rest):
    o_ref = rest[-1]
    d = x_ref.shape[2]
    xn = _norm_modulate(x_ref[0], mod_ref, g_ref, mrow, grow)
    for jj, mode in enumerate(modes):
        acc = _dot(xn, w_ref[:, jj * d:(jj + 1) * d])
        if mode == "sigmoid":
            acc = jax.nn.sigmoid(acc)
        elif mode in ("rope_q", "rope_k"):
            cos_ref, sa_ref, sb_ref = rest[:3]
            acc = _rope(acc, cos_ref[...], sa_ref[...], sb_ref[...])
            if mode == "rope_q":
                acc = acc * Q_SCALE
        o_ref[0, :, jj * d:(jj + 1) * d] = acc.astype(BF16)


def _proj(x, mod, g, w, *, modes, col_chunk0, mrow, grow, mod_row_of_batch, tm, rope=None):
    b, t, d = x.shape
    nch = len(modes)
    assert col_chunk0 % nch == 0
    in_specs = [pl.BlockSpec((1, tm, d), lambda bi, i: (bi, i, 0)),
                pl.BlockSpec((1, N_MOD, d), lambda bi, i: (mod_row_of_batch(bi), 0, 0)),
                pl.BlockSpec(g.shape, lambda bi, i: (0, 0)),
                pl.BlockSpec((d, nch * d), lambda bi, i: (0, col_chunk0 // nch),
                             pipeline_mode=pl.Buffered(1))]
    args = [x, mod, g, w]
    if rope is not None:
        in_specs += [pl.BlockSpec((tm, LANES), lambda bi, i: (i, 0))] * 3
        args += list(rope)
    return pl.pallas_call(
        functools.partial(_proj_kernel, modes, mrow, grow),
        grid=(b, t // tm),
        in_specs=in_specs,
        out_specs=pl.BlockSpec((1, tm, nch * d), lambda bi, i: (bi, i, 0)),
        out_shape=jax.ShapeDtypeStruct((b, t, nch * d), BF16),
        compiler_params=_params(("parallel", "parallel")),
        name="proj",
    )(*args)


def _rope_tables(seq):
    pos = jnp.arange(seq)
    r = (pos // GRID_W).astype(F32)[:, None]
    col = (pos % GRID_W).astype(F32)[:, None]
    half = HEAD_DIM // 2
    inv = ROPE_THETA ** (-jnp.arange(0, half, 2, dtype=F32) / half)
    cr, sr = jnp.cos(r * inv), jnp.sin(r * inv)
    cc, sc = jnp.cos(col * inv), jnp.sin(col * inv)
    z = jnp.zeros_like(sr)
    cos = jnp.concatenate([cr, cr, cc, cc], axis=1)
    sa = jnp.concatenate([z, sr, z, sc], axis=1)
    sb = jnp.concatenate([-sr, z, -sc, z], axis=1)
    rep = LANES // HEAD_DIM
    return tuple(jnp.tile(t, (1, rep)) for t in (cos, sa, sb))


def _dft_tables(seq):
    n = 2 * seq
    n2 = DFT_N2
    n1 = n // n2
    k1h = n1 // 2

    def cs(num, den):
        ang = (2.0 * math.pi / den) * (num % den).astype(F32)
        return jnp.cos(ang), jnp.sin(ang)

    i1 = jnp.arange(n1)
    c, s = cs(i1[:, None] * i1[None, :], n1)
    f1_data = jnp.concatenate([jnp.concatenate([c[:, :k1h], s[:, :k1h]], 1),
                               jnp.concatenate([-s[:, :k1h], c[:, :k1h]], 1)], 0)
    f1_real = jnp.concatenate([c, -s], 0)
    ci, si = c[:k1h, :] / n, s[:k1h, :] / n
    f1_inv = jnp.concatenate([jnp.concatenate([ci, -si], 1),
                              jnp.concatenate([si, ci], 1)], 0)
    i2 = jnp.arange(n2)
    num = i2[None, None, :] * (i1[:, None, None] + n1 * i2[None, :, None])
    c2, s2 = cs(num, n)
    g = jnp.concatenate([jnp.concatenate([c2, s2], 2),
                         jnp.concatenate([-s2, c2], 2)], 1)
    h = jnp.swapaxes(g, 1, 2)
    return (f1_data.astype(BF16), f1_real.astype(BF16), f1_inv.astype(BF16),
            g.astype(BF16), h.astype(BF16))


def _block_rows(m, size):
    return pl.ds(pl.multiple_of(m * size, size), size)


def _dft_stage1(src_rows, f1, r_ref, n1, group):
    ct = r_ref.shape[1]

    def body(i, carry):
        m0 = i * group
        z = jnp.concatenate([src_rows(m0 + j) for j in range(group)], axis=1).astype(BF16)
        r = _dot(f1, z)
        for j in range(group):
            r_ref[_block_rows(m0 + j, 2 * n1), :] = r[:, j * ct:(j + 1) * ct]
        return carry

    lax.fori_loop(0, DFT_N2 // group, body, 0, unroll=2)


def _stage2_operand(r_ref, q, n1):
    return jnp.concatenate([r_ref[pl.ds(q, DFT_N2, stride=2 * n1), :],
                            r_ref[pl.ds(n1 + q, DFT_N2, stride=2 * n1), :]], axis=0).astype(BF16)


def _filt_mlp_kernel(z_ref, w1_ref, b1_ref, w2_ref, b2_ref, w3_ref, b3_ref, fr_ref, o_ref):
    fr = fr_ref[...]
    h = jnp.sin(fr * (_dot(z_ref[...], w1_ref[...], precision=HIGHEST) + b1_ref[...]))
    h = jnp.sin(fr * (_dot(h, w2_ref[...], precision=HIGHEST) + b2_ref[...]))
    o_ref[...] = jnp.sin(fr * (_dot(h, w3_ref[...], precision=HIGHEST) + b3_ref[...]))


def _filt_spec_kernel(seq, n1, group1, group2, h_ref, w4f_ref, w4b_ref, t_ref, dl_ref, f1_ref, g_ref,
                      o_ref, k_ref, r_ref):
    n2 = DFT_N2
    n = 2 * seq
    ct = o_ref.shape[2]
    row = lax.broadcasted_iota(jnp.int32, (n, ct), 0)
    hk = jnp.concatenate([_dot(h_ref[0:seq, :], w4f_ref[...], precision=HIGHEST),
                          _dot(h_ref[seq:n, :], w4b_ref[...], precision=HIGHEST)], axis=0)
    kraw = jnp.where(row == seq, 0.0, hk * jnp.exp(-t_ref[...] * dl_ref[...]))
    k_ref[...] = kraw / (jnp.sum(jnp.abs(kraw), axis=0, keepdims=True) + EPS)

    _dft_stage1(lambda m: k_ref[pl.ds(m, n1, stride=n2), :], f1_ref[...], r_ref, n1, group1)

    def stage2(i, carry):
        q0 = i * group2
        ops = [_stage2_operand(r_ref, q0 + j, n1) for j in range(group2)]
        for j in range(group2):
            o_ref[q0 + j] = _dot(g_ref[q0 + j], ops[j]).astype(o_ref.dtype)
        return carry

    lax.fori_loop(0, n1 // group2, stage2, 0)


def _hyena_filter_spectrum(seq, d_hy, filt, tables, ct):
    w1, b1, w2, b2, w3, b3, w4, freq = filt
    f1_real, g = tables
    n = 2 * seq
    n2 = DFT_N2
    n1 = n // n2
    order = w1.shape[1]
    pad = LANES - order
    bands = (FILTER_EMB - 1) // 2
    t = jnp.linspace(0.0, 1.0, seq, dtype=F32)[:, None]
    w = 2.0 * math.pi * jnp.arange(seq, dtype=F32)[:, None] / seq
    f = jnp.linspace(1e-4, bands - 1, bands, dtype=F32)[None, :]
    z = jnp.concatenate([t, jnp.cos(f * w), -jnp.sin(f * w)], axis=-1)
    def circular(a):
        return jnp.concatenate([a, a[:1], a[:0:-1]], axis=0)

    z_ext = jnp.pad(circular(z), ((0, 0), (0, LANES - FILTER_EMB)))
    t_ext = circular(t)
    deltas = jnp.abs(jnp.linspace(math.log(DECAY_TARGET) / SLOW_DECAY_PCT,
                                  math.log(DECAY_TARGET) / FAST_DECAY_PCT, d_hy, dtype=F32))[None, :]

    def padc(a):
        return jnp.pad(a, ((0, 0), (0, pad)))

    w1p = jnp.pad(w1, ((0, LANES - FILTER_EMB), (0, pad)))
    w2p = jnp.pad(w2, ((0, pad), (0, pad)))
    w3p = jnp.pad(w3, ((0, pad), (0, pad)))
    w4p = jnp.pad(w4, ((0, pad), (0, 0)))
    h3 = pl.pallas_call(
        _filt_mlp_kernel,
        out_shape=jax.ShapeDtypeStruct((n, LANES), F32),
        compiler_params=pltpu.CompilerParams(vmem_limit_bytes=VMEM_LIMIT),
        name="filt_mlp",
    )(z_ext, w1p, padc(b1[None]), w2p, padc(b2[None]), w3p, padc(b3[None]), padc(freq[None]))

    nct = d_hy // ct
    return pl.pallas_call(
        functools.partial(_filt_spec_kernel, seq, n1, SUBLANES, min(STAGE2_GROUP, n1)),
        grid=(nct,),
        in_specs=[pl.BlockSpec((n, LANES), lambda j: (0, 0)),
                  pl.BlockSpec((LANES, ct), lambda j: (0, j)),
                  pl.BlockSpec((LANES, ct), lambda j: (0, nct + j)),
                  pl.BlockSpec((n, 1), lambda j: (0, 0)),
                  pl.BlockSpec((1, ct), lambda j: (0, j)),
                  pl.BlockSpec(f1_real.shape, lambda j: (0, 0)),
                  pl.BlockSpec(g.shape, lambda j: (0, 0, 0))],
        out_specs=pl.BlockSpec((n1, 2 * n2, ct), lambda j: (0, 0, j)),
        out_shape=jax.ShapeDtypeStruct((n1, 2 * n2, d_hy), BF16),
        scratch_shapes=[pltpu.VMEM((n, ct), F32), pltpu.VMEM((n2 * 2 * n1, ct), F32)],
        compiler_params=_params(("arbitrary",)),
        name="filt_spec",
    )(h3, w4p, w4p, t_ext, deltas, f1_real, g)


def _conv3(z, w, b):
    seq = z.shape[0]
    row = lax.broadcasted_iota(jnp.int32, z.shape, 0)
    zm = jnp.where(row == 0, 0.0, pltpu.roll(z, 1, 0))
    zp = jnp.where(row == seq - 1, 0.0, pltpu.roll(z, seq - 1, 0))
    return zm * w[0:1, :] + z * w[1:2, :] + zp * w[2:3, :] + b


def _hyena_kernel(seq, n1, group1, group2, x0_ref, x1_ref, v_ref, cw0_ref, cw1_ref, cwv_ref,
                  cb0_ref, cb1_ref, cbv_ref, bias_ref, f1_ref, f1i_ref, g_ref, h_ref, kf_ref,
                  o_ref, u_ref, r_ref):
    n2 = DFT_N2
    k1h = n1 // 2
    ct = o_ref.shape[2]

    for bb in range(2):
        x1 = _conv3(x1_ref[bb].astype(F32), cw1_ref[...], cb1_ref[...])
        v = _conv3(v_ref[bb].astype(F32), cwv_ref[...], cbv_ref[...])
        u_ref[bb * seq:(bb + 1) * seq, :] = v * x1

    def src_rows(m):
        return jnp.concatenate([u_ref[pl.ds(m, k1h, stride=n2), :],
                                u_ref[pl.ds(seq + m, k1h, stride=n2), :]], axis=0)

    _dft_stage1(src_rows, f1_ref[...], r_ref, n1, group1)

    def stage2(i, carry):
        q0 = i * group2
        ops = [_stage2_operand(r_ref, q0 + j, n1) for j in range(group2)]
        xs = [_dot(g_ref[q0 + j], ops[j]) for j in range(group2)]
        ys = []
        for j in range(group2):
            kf = kf_ref[q0 + j].astype(F32)
            xr, xi, kr, ki = xs[j][:n2], xs[j][n2:], kf[:n2], kf[n2:]
            ys.append(jnp.concatenate([xr * kr - xi * ki, xr * ki + xi * kr], axis=0).astype(BF16))
        outs = [_dot(h_ref[q0 + j], ys[j]) for j in range(group2)]
        for j in range(group2):
            r_ref[pl.ds(q0 + j, n2, stride=2 * n1), :] = outs[j][:n2]
            r_ref[pl.ds(n1 + q0 + j, n2, stride=2 * n1), :] = outs[j][n2:]
        return carry

    lax.fori_loop(0, n1 // group2, stage2, 0)

    f1i = f1i_ref[...]
    bias = bias_ref[...]

    def stage3(i, carry):
        m0 = i * group1
        b = jnp.concatenate([r_ref[_block_rows(m0 + j, 2 * n1), :] for j in range(group1)],
                            axis=1).astype(BF16)
        y = _dot(f1i, b)
        for j in range(group1):
            for bb in range(2):
                rows = pl.ds(bb * seq + m0 + j, k1h, stride=n2)
                u_ref[rows, :] = y[bb * k1h:(bb + 1) * k1h, j * ct:(j + 1) * ct] + u_ref[rows, :] * bias
        return carry

    lax.fori_loop(0, n2 // group1, stage3, 0, unroll=2)

    for bb in range(2):
        x0 = _conv3(x0_ref[bb].astype(F32), cw0_ref[...], cb0_ref[...])
        o_ref[bb] = (u_ref[bb * seq:(bb + 1) * seq, :] * x0).astype(BF16)


def _hyena(p, conv_w, conv_b, hy_bias, kf, tables, d_hy, ct):
    b, seq, _ = p.shape
    f1_data, f1_inv, g, h = tables
    n2 = DFT_N2
    n1 = 2 * seq // n2
    nct = d_hy // ct
    once = pl.Buffered(1)
    zspec = lambda off: pl.BlockSpec((2, seq, ct), lambda j, q: (q, 0, off * nct + j))
    wspec = lambda off: pl.BlockSpec((SHORT_CONV, ct), lambda j, q: (0, off * nct + j))
    bspec = lambda off: pl.BlockSpec((1, ct), lambda j, q: (0, off * nct + j))
    cb = conv_b.reshape(1, -1)
    return pl.pallas_call(
        functools.partial(_hyena_kernel, seq, n1, SUBLANES, min(STAGE2_GROUP, n1)),
        grid=(nct, b // 2),
        in_specs=[zspec(0), zspec(1), zspec(2), wspec(0), wspec(1), wspec(2), bspec(0), bspec(1), bspec(2),
                  pl.BlockSpec((1, ct), lambda j, q: (0, j)),
                  pl.BlockSpec(f1_data.shape, lambda j, q: (0, 0)),
                  pl.BlockSpec(f1_inv.shape, lambda j, q: (0, 0)),
                  pl.BlockSpec(g.shape, lambda j, q: (0, 0, 0), pipeline_mode=once),
                  pl.BlockSpec(h.shape, lambda j, q: (0, 0, 0), pipeline_mode=once),
                  pl.BlockSpec((n1, 2 * n2, ct), lambda j, q: (0, 0, j), pipeline_mode=once)],
        out_specs=pl.BlockSpec((2, seq, ct), lambda j, q: (q, 0, j)),
        out_shape=jax.ShapeDtypeStruct((b, seq, d_hy), BF16),
        scratch_shapes=[pltpu.VMEM((2 * seq, ct), F32), pltpu.VMEM((n2 * 2 * n1, ct), F32)],
        compiler_params=_params(("arbitrary", "arbitrary")),
        name="hyena",
    )(p, p, p, conv_w, conv_w, conv_w, cb, cb, cb, hy_bias.reshape(1, -1), f1_data, f1_inv, g, h, kf)


def _attn_kernel(lam_init, chunks, tq, q_ref, k_ref, v_ref, kc_ref, vc_ref, lam_ref, g_ref, o_ref, vx_ref):
    seq = k_ref.shape[1]
    lctx = kc_ref.shape[1]
    hw = q_ref.shape[2]
    nt = (((1,), (1,)), ((), ()))

    vx_ref[0:seq, 0:hw] = v_ref[0]
    vx_ref[seq:seq + lctx, 0:hw] = vc_ref[0]
    vx_ref[:, hw:2 * hw] = jnp.ones((seq + lctx, hw), BF16)

    lp = lam_ref[...]
    lam = (jnp.exp(jnp.sum(lp[0:1] * lp[1:2], axis=-1, keepdims=True))
           - jnp.exp(jnp.sum(lp[2:3] * lp[3:4], axis=-1, keepdims=True)) + lam_init)

    def q_tile(i, carry):
        rows = pl.ds(pl.multiple_of(i * tq, tq), tq)
        q = q_ref[0, rows, :]
        lane = lax.broadcasted_iota(jnp.int32, q.shape, 1)
        zero = jnp.zeros_like(q)
        qq = jnp.concatenate([jnp.where(lane < HEAD_DIM, q, zero), jnp.where(lane >= HEAD_DIM, q, zero)],
                             axis=0)

        def scores(lo, hi):
            parts = []
            if lo < seq:
                parts.append(lax.dot_general(qq, k_ref[0, lo:min(hi, seq), :], nt, preferred_element_type=F32))
            if hi > seq:
                parts.append(lax.dot_general(qq, kc_ref[0, max(lo, seq) - seq:hi - seq, :], nt,
                                             preferred_element_type=F32))
            return parts[0] if len(parts) == 1 else jnp.concatenate(parts, axis=1)

        m = acc = None
        s_next = scores(*chunks[0])
        for ci, (lo, hi) in enumerate(chunks):
            s = s_next
            if ci + 1 < len(chunks):
                s_next = scores(*chunks[ci + 1])
            mc = jnp.max(s, axis=-1, keepdims=True)
            m_new = mc if m is None else jnp.maximum(m, mc)
            pv = _dot(jnp.exp2(s - m_new).astype(BF16), vx_ref[lo:hi, :])
            acc = pv if acc is None else acc * jnp.exp2(m - m_new) + pv
            m = m_new

        o12 = acc[:, :hw] / acc[:, hw:]
        o = o12[:tq] - lam * o12[tq:]
        o_ref[0, rows, :] = (_rms(o, g_ref[...]) * (1.0 - lam_init)).astype(BF16)
        return carry

    lax.fori_loop(0, seq // tq, q_tile, 0, unroll=4)


def _attn(p, pc, lam_rows, subln_g, lam_init, d_hy, tq, tk):
    b, seq, _ = p.shape
    lctx = pc.shape[1]
    hw = 2 * HEAD_DIM
    c0 = 3 * d_hy // hw
    nh = N_HEADS
    edges = list(range(0, seq, tk)) + [seq + lctx]
    chunks = tuple(zip(edges[:-1], edges[1:]))
    return pl.pallas_call(
        functools.partial(_attn_kernel, lam_init, chunks, tq),
        grid=(b, nh),
        in_specs=[pl.BlockSpec((1, seq, hw), lambda bi, hh: (bi, 0, c0 + hh)),
                  pl.BlockSpec((1, seq, hw), lambda bi, hh: (bi, 0, c0 + nh + hh)),
                  pl.BlockSpec((1, seq, hw), lambda bi, hh: (bi, 0, c0 + 2 * nh + hh)),
                  pl.BlockSpec((1, lctx, hw), lambda bi, hh: (bi, 0, hh)),
                  pl.BlockSpec((1, lctx, hw), lambda bi, hh: (bi, 0, nh + hh)),
                  pl.BlockSpec(lam_rows.shape, lambda bi, hh: (0, 0)),
                  pl.BlockSpec((1, hw), lambda bi, hh: (0, 0))],
        out_specs=pl.BlockSpec((1, seq, hw), lambda bi, hh: (bi, 0, hh)),
        out_shape=jax.ShapeDtypeStruct((b, seq, nh * hw), BF16),
        scratch_shapes=[pltpu.VMEM((seq + lctx, 2 * hw), BF16)],
        compiler_params=_params(("parallel", "parallel")),
        name="attn",
    )(p, p, p, pc, pc, lam_rows, subln_g.reshape(1, hw))


def _merge_kernel(mrow, grow, s_ref, yh_ref, od_ref, gh_ref, gd_ref, mod_ref, g_ref,
                  why_ref, wda_ref, wo_ref, o_ref):
    y_hy = _dot(yh_ref[0], why_ref[...])
    y_da = _dot(od_ref[0], wda_ref[...])
    mix = gh_ref[0].astype(F32) * y_hy + gd_ref[0].astype(F32) * y_da
    y = _dot(mix.astype(BF16), wo_ref[...])
    o_ref[0] = s_ref[0] + mod_ref[0, mrow:mrow + 1, :] * _rms(y, g_ref[grow:grow + 1, :])


def _merge(s, yh, od, p, mod, g, w_hy_out, w_da_out, w_o, *, mrow, grow, gate_chunk0, tm):
    b, t, d = s.shape
    tok = lambda c: pl.BlockSpec((1, tm, d), lambda bi, i: (bi, i, c))
    wfull = lambda w: pl.BlockSpec(w.shape, lambda bi, i: (0, 0))
    return pl.pallas_call(
        functools.partial(_merge_kernel, mrow, grow),
        grid=(b, t // tm),
        in_specs=[tok(0), tok(0), tok(0), tok(gate_chunk0), tok(gate_chunk0 + 1),
                  pl.BlockSpec((1, N_MOD, d), lambda bi, i: (bi, 0, 0)),
                  pl.BlockSpec(g.shape, lambda bi, i: (0, 0)),
                  wfull(w_hy_out), wfull(w_da_out), wfull(w_o)],
        out_specs=tok(0),
        out_shape=jax.ShapeDtypeStruct((b, t, d), F32),
        compiler_params=_params(("parallel", "parallel")),
        name="merge",
    )(s, yh, od, p, p, mod, g, w_hy_out, w_da_out, w_o)


def kernel(x, c, ctx, c_ctx, w_ada, b_ada, norm_g, w_ff_in, w_ff_out, w_in, hy_conv_w, hy_conv_b, filt_w1, filt_b1, filt_w2, filt_b2, filt_w3, filt_b3, filt_w4, filt_freq, hy_bias, lambda_q1, lambda_k1, lambda_q2, lambda_k2, subln_g, w_hy_out, w_da_out, w_o):
    b, seq, d = x.shape
    lctx = ctx.shape[1]
    depth = w_ada.shape[0]
    d_hy = hy_bias.shape[1]
    assert depth == 1 and b % 2 == 0 and b < 16
    ctx_row = b
    tm = min(512, seq)
    tq = min(256, seq)
    tk = min(1536, seq)
    ct = LANES

    rope = _rope_tables(seq)
    f1_data, f1_real, f1_inv, dft_g, dft_h = _dft_tables(seq)
    c16 = jnp.zeros((16, d), F32).at[:b].set(c).at[ctx_row].set(c_ctx)
    latent_row = lambda bi: bi
    context_row = lambda bi: ctx_row
    modes = ("plain",) * (3 * d_hy // d) + ("rope_q", "rope_k", "plain", "sigmoid", "sigmoid")
    kv_chunk0 = 3 * d_hy // d + 1

    xs, cs = x, ctx
    for l in range(depth):
        lam_init = 0.8 - 0.6 * math.exp(-0.3 * l)
        mod = _ada(c16, w_ada[l], b_ada[l]).reshape(16, N_MOD, d)
        g = norm_g[l]
        wfi = w_ff_in[l].astype(BF16)
        wfo = w_ff_out[l].astype(BF16)
        wi = w_in[l].astype(BF16)
        filt = (filt_w1[l], filt_b1[l], filt_w2[l], filt_b2[l], filt_w3[l], filt_b3[l],
                filt_w4[l], filt_freq[l])
        lam_rows = jnp.stack([lambda_q1[l], lambda_k1[l], lambda_q2[l], lambda_k2[l]])

        xs = _ffn(xs, mod, g, wfi[0], wfo[0], mrow=0, grow=0, mod_row_of_batch=latent_row, tm=tm)
        cs = _ffn(cs, mod, g, wfi[0], wfo[0], mrow=0, grow=0, mod_row_of_batch=context_row,
                  tm=min(tm, lctx))

        p = _proj(xs, mod, g, wi, modes=modes, col_chunk0=0, mrow=3, grow=2,
                  mod_row_of_batch=latent_row, tm=tm, rope=rope)
        pc = _proj(cs, mod, g, wi, modes=("plain", "plain"), col_chunk0=kv_chunk0, mrow=3, grow=2,
                   mod_row_of_batch=context_row, tm=min(tm, lctx))

        kf = _hyena_filter_spectrum(seq, d_hy, filt, (f1_real, dft_g), ct)
        yh = _hyena(p, hy_conv_w[l], hy_conv_b[l], hy_bias[l], kf, (f1_data, f1_inv, dft_g, dft_h), d_hy, ct)
        od = _attn(p, pc, lam_rows, subln_g[l], lam_init, d_hy, tq, tk)
        xs = _merge(xs, yh, od, p, mod, g, w_hy_out[l].astype(BF16), w_da_out[l].astype(BF16),
                    w_o[l].astype(BF16), mrow=5, grow=3, gate_chunk0=kv_chunk0 + 2, tm=tm)
        xs = _ffn(xs, mod, g, wfi[1], wfo[1], mrow=6, grow=4, mod_row_of_batch=latent_row, tm=tm)
    return xs
```

```python
import functools
import math

import jax
import jax.numpy as jnp
from jax import lax
from jax.experimental import pallas as pl
from jax.experimental.pallas import tpu as pltpu

F32 = jnp.float32
BF16 = jnp.bfloat16
HIGHEST = lax.Precision.HIGHEST

N_HEADS = 8
HEAD_DIM = 64
GRID_W = 64
ROPE_THETA = 10000.0
SHORT_CONV = 3
FILTER_EMB = 33
DECAY_TARGET = 1e-2
FAST_DECAY_PCT = 0.3
SLOW_DECAY_PCT = 1.5
EPS = 1e-6
N_MOD = 9
Q_SCALE = HEAD_DIM ** -0.5 * math.log2(math.e)

LANES = 128
SUBLANES = 8
MXU_DIM = 256
DFT_N2 = 128
DFT_GROUP = 16
STAGE2_GROUP = 8
VMEM_LIMIT = 56 * 1024 * 1024


def _params(sem, vmem=VMEM_LIMIT):
    return pltpu.CompilerParams(dimension_semantics=sem, vmem_limit_bytes=vmem)


def _rms(x, g):
    return x * lax.rsqrt(jnp.mean(x * x, axis=-1, keepdims=True) + EPS) * g


def _dot(a, b, **kw):
    return jnp.dot(a, b, preferred_element_type=F32, **kw)


def _norm_modulate(x, mod_ref, g_ref, mrow, grow):
    shift = mod_ref[0, mrow:mrow + 1, :]
    scale = mod_ref[0, mrow + 1:mrow + 2, :]
    return (_rms(x, g_ref[grow:grow + 1, :]) * (1.0 + scale) + shift).astype(BF16)


def _ada_kernel(c_ref, w_ref, b_ref, o_ref):
    c = c_ref[...]
    a = c * jax.nn.sigmoid(c)
    o_ref[...] = _dot(a, w_ref[...], precision=HIGHEST) + b_ref[...]


def _ada(c16, w_ada, b_ada):
    d = c16.shape[1]
    n = w_ada.shape[1]
    return pl.pallas_call(
        _ada_kernel,
        grid=(n // d,),
        in_specs=[pl.BlockSpec((16, d), lambda j: (0, 0)),
                  pl.BlockSpec((d, d), lambda j: (0, j)),
                  pl.BlockSpec((1, d), lambda j: (0, j))],
        out_specs=pl.BlockSpec((16, d), lambda j: (0, j)),
        out_shape=jax.ShapeDtypeStruct((16, n), F32),
        compiler_params=_params(("arbitrary",)),
        name="ada",
    )(c16, w_ada, b_ada.reshape(1, n))


def _ffn_kernel(mrow, grow, bounds, s_ref, mod_ref, g_ref, wi_ref, wo_ref, o_ref):
    dff = wo_ref.shape[0]
    x = s_ref[0]
    xn = _norm_modulate(x, mod_ref, g_ref, mrow, grow)
    acc = None
    for lo, hi in zip(bounds[:-1], bounds[1:]):
        hg = _dot(xn, wi_ref[:, lo:hi])
        hu = _dot(xn, wi_ref[:, dff + lo:dff + hi])
        h = (hg * jax.nn.sigmoid(hg) * hu).astype(BF16)
        part = _dot(h, wo_ref[lo:hi, :])
        acc = part if acc is None else acc + part
    gate = mod_ref[0, mrow + 2:mrow + 3, :]
    o_ref[0] = x + 0.5 * gate * _rms(acc, g_ref[grow + 1:grow + 2, :])


def _ffn(s, mod, g, w_in, w_out, *, mrow, grow, mod_row_of_batch, tm):
    b, t, d = s.shape
    dff = w_out.shape[0]
    half = (dff // MXU_DIM + 1) // 2 * MXU_DIM
    bounds = (0, half, dff)
    once = pl.Buffered(1)
    return pl.pallas_call(
        functools.partial(_ffn_kernel, mrow, grow, bounds),
        grid=(b, t // tm),
        in_specs=[pl.BlockSpec((1, tm, d), lambda bi, i: (bi, i, 0)),
                  pl.BlockSpec((1, N_MOD, d), lambda bi, i: (mod_row_of_batch(bi), 0, 0)),
                  pl.BlockSpec(g.shape, lambda bi, i: (0, 0)),
                  pl.BlockSpec(w_in.shape, lambda bi, i: (0, 0), pipeline_mode=once),
                  pl.BlockSpec(w_out.shape, lambda bi, i: (0, 0), pipeline_mode=once)],
        out_specs=pl.BlockSpec((1, tm, d), lambda bi, i: (bi, i, 0)),
        out_shape=jax.ShapeDtypeStruct((b, t, d), F32),
        compiler_params=_params(("parallel", "parallel")),
        name="ffn",
    )(s, mod, g, w_in, w_out)


def _rope(acc, cos, sa, sb):
    cols = []
    for k in range(acc.shape[1] // LANES):
        blk = acc[:, k * LANES:(k + 1) * LANES]
        cols.append(blk * cos + pltpu.roll(blk, 16, 1) * sa + pltpu.roll(blk, LANES - 16, 1) * sb)
    return jnp.concatenate(cols, axis=1)


def _proj_kernel(modes, mrow, grow, x_ref, mod_ref, g_ref, w_ref, *rest):
    o_ref = rest[-1]
    d = x_ref.shape[2]
    xn = _norm_modulate(x_ref[0], mod_ref, g_ref, mrow, grow)
    for jj, mode in enumerate(modes):
        acc = _dot(xn, w_ref[:, jj * d:(jj + 1) * d])
        if mode == "sigmoid":
            acc = jax.nn.sigmoid(acc)
        elif mode in ("rope_q", "rope_k"):
            cos_ref, sa_ref, sb_ref = rest[:3]
            acc = _rope(acc, cos_ref[...], sa_ref[...], sb_ref[...])
            if mode == "rope_q":
                acc = acc * Q_SCALE
        o_ref[0, :, jj * d:(jj + 1) * d] = acc.astype(BF16)


def _proj(x, mod, g, w, *, modes, col_chunk0, mrow, grow, mod_row_of_batch, tm, rope=None):
    b, t, d = x.shape
    nch = len(modes)
    assert col_chunk0 % nch == 0
    in_specs = [pl.BlockSpec((1, tm, d), lambda bi, i: (bi, i, 0)),
                pl.BlockSpec((1, N_MOD, d), lambda bi, i: (mod_row_of_batch(bi), 0, 0)),
                pl.BlockSpec(g.shape, lambda bi, i: (0, 0)),
                pl.BlockSpec((d, nch * d), lambda bi, i: (0, col_chunk0 // nch),
                             pipeline_mode=pl.Buffered(1))]
    args = [x, mod, g, w]
    if rope is not None:
        in_specs += [pl.BlockSpec((tm, LANES), lambda bi, i: (i, 0))] * 3
        args += list(rope)
    return pl.pallas_call(
        functools.partial(_proj_kernel, modes, mrow, grow),
        grid=(b, t // tm),
        in_specs=in_specs,
        out_specs=pl.BlockSpec((1, tm, nch * d), lambda bi, i: (bi, i, 0)),
        out_shape=jax.ShapeDtypeStruct((b, t, nch * d), BF16),
        compiler_params=_params(("parallel", "parallel")),
        name="proj",
    )(*args)


def _rope_tables(seq):
    pos = jnp.arange(seq)
    r = (pos // GRID_W).astype(F32)[:, None]
    col = (pos % GRID_W).astype(F32)[:, None]
    half = HEAD_DIM // 2
    inv = ROPE_THETA ** (-jnp.arange(0, half, 2, dtype=F32) / half)
    cr, sr = jnp.cos(r * inv), jnp.sin(r * inv)
    cc, sc = jnp.cos(col * inv), jnp.sin(col * inv)
    z = jnp.zeros_like(sr)
    cos = jnp.concatenate([cr, cr, cc, cc], axis=1)
    sa = jnp.concatenate([z, sr, z, sc], axis=1)
    sb = jnp.concatenate([-sr, z, -sc, z], axis=1)
    rep = LANES // HEAD_DIM
    return tuple(jnp.tile(t, (1, rep)) for t in (cos, sa, sb))


def _dft_tables(seq):
    n = 2 * seq
    n2 = DFT_N2
    n1 = n // n2
    k1h = n1 // 2

    def cs(num, den):
        ang = (2.0 * math.pi / den) * (num % den).astype(F32)
        return jnp.cos(ang), jnp.sin(ang)

    i1 = jnp.arange(n1)
    c, s = cs(i1[:, None] * i1[None, :], n1)
    f1_data = jnp.concatenate([jnp.concatenate([c[:, :k1h], s[:, :k1h]], 1),
                               jnp.concatenate([-s[:, :k1h], c[:, :k1h]], 1)], 0)
    f1_real = jnp.concatenate([c, -s], 0)
    ci, si = c[:k1h, :] / n, s[:k1h, :] / n
    f1_inv = jnp.concatenate([jnp.concatenate([ci, -si], 1),
                              jnp.concatenate([si, ci], 1)], 0)
    i2 = jnp.arange(n2)
    num = i2[None, None, :] * (i1[:, None, None] + n1 * i2[None, :, None])
    c2, s2 = cs(num, n)
    g = jnp.concatenate([jnp.concatenate([c2, s2], 2),
                         jnp.concatenate([-s2, c2], 2)], 1)
    h = jnp.swapaxes(g, 1, 2)
    return (f1_data.astype(BF16), f1_real.astype(BF16), f1_inv.astype(BF16),
            g.astype(BF16), h.astype(BF16))


def _lane_blocks_to_rows(x, group):
    ct = x.shape[1] // group
    return pltpu.einshape("abc->bac", jnp.stack([x[:, j * ct:(j + 1) * ct] for j in range(group)]))


def _rows_to_lane_blocks(x3):
    xt = pltpu.einshape("abc->bac", x3)
    return jnp.concatenate([xt[j] for j in range(xt.shape[0])], axis=1)


def _dft_stage1(x_ref, f1, t_ref, group):
    def body(i, carry):
        rows = pl.ds(pl.multiple_of(i * group, group), group)
        z = _rows_to_lane_blocks(x_ref[:, rows, :]).astype(BF16)
        t_ref[:, rows, :] = _lane_blocks_to_rows(_dot(f1, z), group).astype(t_ref.dtype)
        return carry

    lax.fori_loop(0, DFT_N2 // group, body, 0)


def _stage2_operand(t_ref, q, n1):
    return jnp.concatenate([t_ref[q], t_ref[n1 + q]], axis=0)


def _filt_mlp_kernel(z_ref, w1_ref, b1_ref, w2_ref, b2_ref, w3_ref, b3_ref, fr_ref, o_ref):
    fr = fr_ref[...]
    h = jnp.sin(fr * (_dot(z_ref[...], w1_ref[...], precision=HIGHEST) + b1_ref[...]))
    h = jnp.sin(fr * (_dot(h, w2_ref[...], precision=HIGHEST) + b2_ref[...]))
    o_ref[...] = jnp.sin(fr * (_dot(h, w3_ref[...], precision=HIGHEST) + b3_ref[...]))


def _filt_spec_kernel(seq, n1, group1, group2, h_ref, w4f_ref, w4b_ref, t_ref, dl_ref, f1_ref, g_ref,
                      o_ref, k_ref, r_ref):
    n2 = DFT_N2
    n = 2 * seq
    ct = o_ref.shape[2]
    row = lax.broadcasted_iota(jnp.int32, (n, ct), 0)
    hk = jnp.concatenate([_dot(h_ref[0:seq, :], w4f_ref[...], precision=HIGHEST),
                          _dot(h_ref[seq:n, :], w4b_ref[...], precision=HIGHEST)], axis=0)
    kraw = jnp.where(row == seq, 0.0, hk * jnp.exp(-t_ref[...] * dl_ref[...]))
    k_ref[...] = (kraw / (jnp.sum(jnp.abs(kraw), axis=0, keepdims=True) + EPS)).reshape(n1, n2, ct)

    _dft_stage1(k_ref, f1_ref[...], r_ref, group1)

    def stage2(i, carry):
        q0 = i * group2
        ops = [_stage2_operand(r_ref, q0 + j, n1) for j in range(group2)]
        for j in range(group2):
            o_ref[q0 + j] = _dot(g_ref[q0 + j], ops[j]).astype(o_ref.dtype)
        return carry

    lax.fori_loop(0, n1 // group2, stage2, 0)


def _hyena_filter_spectrum(seq, d_hy, filt, tables, ct):
    w1, b1, w2, b2, w3, b3, w4, freq = filt
    f1_real, g = tables
    n = 2 * seq
    n2 = DFT_N2
    n1 = n // n2
    order = w1.shape[1]
    pad = LANES - order
    bands = (FILTER_EMB - 1) // 2
    t = jnp.linspace(0.0, 1.0, seq, dtype=F32)[:, None]
    w = 2.0 * math.pi * jnp.arange(seq, dtype=F32)[:, None] / seq
    f = jnp.linspace(1e-4, bands - 1, bands, dtype=F32)[None, :]
    z = jnp.concatenate([t, jnp.cos(f * w), -jnp.sin(f * w)], axis=-1)
    def circular(a):
        return jnp.concatenate([a, a[:1], a[:0:-1]], axis=0)

    z_ext = jnp.pad(circular(z), ((0, 0), (0, LANES - FILTER_EMB)))
    t_ext = circular(t)
    deltas = jnp.abs(jnp.linspace(math.log(DECAY_TARGET) / SLOW_DECAY_PCT,
                                  math.log(DECAY_TARGET) / FAST_DECAY_PCT, d_hy, dtype=F32))[None, :]

    def padc(a):
        return jnp.pad(a, ((0, 0), (0, pad)))

    w1p = jnp.pad(w1, ((0, LANES - FILTER_EMB), (0, pad)))
    w2p = jnp.pad(w2, ((0, pad), (0, pad)))
    w3p = jnp.pad(w3, ((0, pad), (0, pad)))
    w4p = jnp.pad(w4, ((0, pad), (0, 0)))
    h3 = pl.pallas_call(
        _filt_mlp_kernel,
        out_shape=jax.ShapeDtypeStruct((n, LANES), F32),
        compiler_params=pltpu.CompilerParams(vmem_limit_bytes=VMEM_LIMIT),
        name="filt_mlp",
    )(z_ext, w1p, padc(b1[None]), w2p, padc(b2[None]), w3p, padc(b3[None]), padc(freq[None]))

    nct = d_hy // ct
    return pl.pallas_call(
        functools.partial(_filt_spec_kernel, seq, n1, DFT_GROUP, min(STAGE2_GROUP, n1)),
        grid=(nct,),
        in_specs=[pl.BlockSpec((n, LANES), lambda j: (0, 0)),
                  pl.BlockSpec((LANES, ct), lambda j: (0, j)),
                  pl.BlockSpec((LANES, ct), lambda j: (0, nct + j)),
                  pl.BlockSpec((n, 1), lambda j: (0, 0)),
                  pl.BlockSpec((1, ct), lambda j: (0, j)),
                  pl.BlockSpec(f1_real.shape, lambda j: (0, 0)),
                  pl.BlockSpec(g.shape, lambda j: (0, 0, 0))],
        out_specs=pl.BlockSpec((n1, 2 * n2, ct), lambda j: (0, 0, j)),
        out_shape=jax.ShapeDtypeStruct((n1, 2 * n2, d_hy), BF16),
        scratch_shapes=[pltpu.VMEM((n1, n2, ct), F32), pltpu.VMEM((2 * n1, n2, ct), BF16)],
        compiler_params=_params(("arbitrary",)),
        name="filt_spec",
    )(h3, w4p, w4p, t_ext, deltas, f1_real, g)


def _conv3(z, w, b):
    seq = z.shape[0]
    row = lax.broadcasted_iota(jnp.int32, z.shape, 0)
    zm = jnp.where(row == 0, 0.0, pltpu.roll(z, 1, 0))
    zp = jnp.where(row == seq - 1, 0.0, pltpu.roll(z, seq - 1, 0))
    return zm * w[0:1, :] + z * w[1:2, :] + zp * w[2:3, :] + b


def _hyena_kernel(seq, n1, group1, group2, x0_ref, x1_ref, v_ref, cw0_ref, cw1_ref, cwv_ref,
                  cb0_ref, cb1_ref, cbv_ref, bias_ref, f1_ref, f1i_ref, g_ref, h_ref, kf_ref,
                  o_ref, u_ref, r_ref):
    n2 = DFT_N2
    k1h = n1 // 2
    ct = o_ref.shape[2]

    for bb in range(2):
        x1 = _conv3(x1_ref[bb].astype(F32), cw1_ref[...], cb1_ref[...])
        v = _conv3(v_ref[bb].astype(F32), cwv_ref[...], cbv_ref[...])
        u_ref[bb * k1h:(bb + 1) * k1h] = (v * x1).reshape(k1h, n2, ct)

    _dft_stage1(u_ref, f1_ref[...], r_ref, group1)

    def stage2(i, carry):
        q0 = i * group2
        ops = [_stage2_operand(r_ref, q0 + j, n1) for j in range(group2)]
        xs = [_dot(g_ref[q0 + j], ops[j]) for j in range(group2)]
        ys = []
        for j in range(group2):
            kf = kf_ref[q0 + j].astype(F32)
            xr, xi, kr, ki = xs[j][:n2], xs[j][n2:], kf[:n2], kf[n2:]
            ys.append(jnp.concatenate([xr * kr - xi * ki, xr * ki + xi * kr], axis=0).astype(BF16))
        outs = [_dot(h_ref[q0 + j], ys[j]) for j in range(group2)]
        for j in range(group2):
            r_ref[q0 + j] = outs[j][:n2].astype(r_ref.dtype)
            r_ref[n1 + q0 + j] = outs[j][n2:].astype(r_ref.dtype)
        return carry

    lax.fori_loop(0, n1 // group2, stage2, 0)

    f1i = f1i_ref[...]
    bias = bias_ref[...]

    def stage3(i, carry):
        rows = pl.ds(pl.multiple_of(i * group1, group1), group1)
        b = _rows_to_lane_blocks(r_ref[:, rows, :].astype(F32)).astype(BF16)
        y = _lane_blocks_to_rows(_dot(f1i, b), group1)
        u_ref[:, rows, :] = y + u_ref[:, rows, :] * bias
        return carry

    lax.fori_loop(0, n2 // group1, stage3, 0)

    for bb in range(2):
        x0 = _conv3(x0_ref[bb].astype(F32), cw0_ref[...], cb0_ref[...])
        o_ref[bb] = (u_ref[bb * k1h:(bb + 1) * k1h].reshape(seq, ct) * x0).astype(BF16)


def _hyena(p, conv_w, conv_b, hy_bias, kf, tables, d_hy, ct):
    b, seq, _ = p.shape
    f1_data, f1_inv, g, h = tables
    n2 = DFT_N2
    n1 = 2 * seq // n2
    nct = d_hy // ct
    once = pl.Buffered(1)
    zspec = lambda off: pl.BlockSpec((2, seq, ct), lambda j, q: (q, 0, off * nct + j))
    wspec = lambda off: pl.BlockSpec((SHORT_CONV, ct), lambda j, q: (0, off * nct + j))
    bspec = lambda off: pl.BlockSpec((1, ct), lambda j, q: (0, off * nct + j))
    cb = conv_b.reshape(1, -1)
    return pl.pallas_call(
        functools.partial(_hyena_kernel, seq, n1, DFT_GROUP, min(STAGE2_GROUP, n1)),
        grid=(nct, b // 2),
        in_specs=[zspec(0), zspec(1), zspec(2), wspec(0), wspec(1), wspec(2), bspec(0), bspec(1), bspec(2),
                  pl.BlockSpec((1, ct), lambda j, q: (0, j)),
                  pl.BlockSpec(f1_data.shape, lambda j, q: (0, 0)),
                  pl.BlockSpec(f1_inv.shape, lambda j, q: (0, 0)),
                  pl.BlockSpec(g.shape, lambda j, q: (0, 0, 0), pipeline_mode=once),
                  pl.BlockSpec(h.shape, lambda j, q: (0, 0, 0), pipeline_mode=once),
                  pl.BlockSpec((n1, 2 * n2, ct), lambda j, q: (0, 0, j), pipeline_mode=once)],
        out_specs=pl.BlockSpec((2, seq, ct), lambda j, q: (q, 0, j)),
        out_shape=jax.ShapeDtypeStruct((b, seq, d_hy), BF16),
        scratch_shapes=[pltpu.VMEM((n1, n2, ct), F32), pltpu.VMEM((2 * n1, n2, ct), BF16)],
        compiler_params=_params(("arbitrary", "arbitrary")),
        name="hyena",
    )(p, p, p, conv_w, conv_w, conv_w, cb, cb, cb, hy_bias.reshape(1, -1), f1_data, f1_inv, g, h, kf)


def _attn_kernel(lam_init, chunks, tq, q_ref, k_ref, v_ref, kc_ref, vc_ref, lam_ref, g_ref, o_ref, vx_ref):
    seq = k_ref.shape[1]
    lctx = kc_ref.shape[1]
    hw = q_ref.shape[2]
    nt = (((1,), (1,)), ((), ()))

    vx_ref[0:seq, 0:hw] = v_ref[0]
    vx_ref[seq:seq + lctx, 0:hw] = vc_ref[0]
    vx_ref[:, hw:2 * hw] = jnp.ones((seq + lctx, hw), BF16)

    lp = lam_ref[...]
    lam = (jnp.exp(jnp.sum(lp[0:1] * lp[1:2], axis=-1, keepdims=True))
           - jnp.exp(jnp.sum(lp[2:3] * lp[3:4], axis=-1, keepdims=True)) + lam_init)

    def q_tile(i, carry):
        rows = pl.ds(pl.multiple_of(i * tq, tq), tq)
        q = q_ref[0, rows, :]
        lane = lax.broadcasted_iota(jnp.int32, q.shape, 1)
        zero = jnp.zeros_like(q)
        qq = jnp.concatenate([jnp.where(lane < HEAD_DIM, q, zero), jnp.where(lane >= HEAD_DIM, q, zero)],
                             axis=0)

        def scores(lo, hi):
            parts = []
            if lo < seq:
                parts.append(lax.dot_general(qq, k_ref[0, lo:min(hi, seq), :], nt, preferred_element_type=F32))
            if hi > seq:
                parts.append(lax.dot_general(qq, kc_ref[0, max(lo, seq) - seq:hi - seq, :], nt,
                                             preferred_element_type=F32))
            return parts[0] if len(parts) == 1 else jnp.concatenate(parts, axis=1)

        m = acc = None
        s_next = scores(*chunks[0])
        for ci, (lo, hi) in enumerate(chunks):
            s = s_next
            if ci + 1 < len(chunks):
                s_next = scores(*chunks[ci + 1])
            mc = jnp.max(s, axis=-1, keepdims=True)
            m_new = mc if m is None else jnp.maximum(m, mc)
            pv = _dot(jnp.exp2(s - m_new).astype(BF16), vx_ref[lo:hi, :])
            acc = pv if acc is None else acc * jnp.exp2(m - m_new) + pv
            m = m_new

        o12 = acc[:, :hw] / acc[:, hw:]
        o = o12[:tq] - lam * o12[tq:]
        o_ref[0, rows, :] = (_rms(o, g_ref[...]) * (1.0 - lam_init)).astype(BF16)
        return carry

    lax.fori_loop(0, seq // tq, q_tile, 0, unroll=4)


def _attn(p, pc, lam_rows, subln_g, lam_init, d_hy, tq, tk):
    b, seq, _ = p.shape
    lctx = pc.shape[1]
    hw = 2 * HEAD_DIM
    c0 = 3 * d_hy // hw
    nh = N_HEADS
    edges = list(range(0, seq, tk)) + [seq + lctx]
    chunks = tuple(zip(edges[:-1], edges[1:]))
    return pl.pallas_call(
        functools.partial(_attn_kernel, lam_init, chunks, tq),
        grid=(b, nh),
        in_specs=[pl.BlockSpec((1, seq, hw), lambda bi, hh: (bi, 0, c0 + hh)),
                  pl.BlockSpec((1, seq, hw), lambda bi, hh: (bi, 0, c0 + nh + hh)),
                  pl.BlockSpec((1, seq, hw), lambda bi, hh: (bi, 0, c0 + 2 * nh + hh)),
                  pl.BlockSpec((1, lctx, hw), lambda bi, hh: (bi, 0, hh)),
                  pl.BlockSpec((1, lctx, hw), lambda bi, hh: (bi, 0, nh + hh)),
                  pl.BlockSpec(lam_rows.shape, lambda bi, hh: (0, 0)),
                  pl.BlockSpec((1, hw), lambda bi, hh: (0, 0))],
        out_specs=pl.BlockSpec((1, seq, hw), lambda bi, hh: (bi, 0, hh)),
        out_shape=jax.ShapeDtypeStruct((b, seq, nh * hw), BF16),
        scratch_shapes=[pltpu.VMEM((seq + lctx, 2 * hw), BF16)],
        compiler_params=_params(("parallel", "parallel")),
        name="attn",
    )(p, p, p, pc, pc, lam_rows, subln_g.reshape(1, hw))


def _merge_kernel(mrow, grow, s_ref, yh_ref, od_ref, gh_ref, gd_ref, mod_ref, g_ref,
                  why_ref, wda_ref, wo_ref, o_ref):
    y_hy = _dot(yh_ref[0], why_ref[...])
    y_da = _dot(od_ref[0], wda_ref[...])
    mix = gh_ref[0].astype(F32) * y_hy + gd_ref[0].astype(F32) * y_da
    y = _dot(mix.astype(BF16), wo_ref[...])
    o_ref[0] = s_ref[0] + mod_ref[0, mrow:mrow + 1, :] * _rms(y, g_ref[grow:grow + 1, :])


def _merge(s, yh, od, p, mod, g, w_hy_out, w_da_out, w_o, *, mrow, grow, gate_chunk0, tm):
    b, t, d = s.shape
    tok = lambda c: pl.BlockSpec((1, tm, d), lambda bi, i: (bi, i, c))
    wfull = lambda w: pl.BlockSpec(w.shape, lambda bi, i: (0, 0))
    return pl.pallas_call(
        functools.partial(_merge_kernel, mrow, grow),
        grid=(b, t // tm),
        in_specs=[tok(0), tok(0), tok(0), tok(gate_chunk0), tok(gate_chunk0 + 1),
                  pl.BlockSpec((1, N_MOD, d), lambda bi, i: (bi, 0, 0)),
                  pl.BlockSpec(g.shape, lambda bi, i: (0, 0)),
                  wfull(w_hy_out), wfull(w_da_out), wfull(w_o)],
        out_specs=tok(0),
        out_shape=jax.ShapeDtypeStruct((b, t, d), F32),
        compiler_params=_params(("parallel", "parallel")),
        name="merge",
    )(s, yh, od, p, p, mod, g, w_hy_out, w_da_out, w_o)


def kernel(x, c, ctx, c_ctx, w_ada, b_ada, norm_g, w_ff_in, w_ff_out, w_in, hy_conv_w, hy_conv_b, filt_w1, filt_b1, filt_w2, filt_b2, filt_w3, filt_b3, filt_w4, filt_freq, hy_bias, lambda_q1, lambda_k1, lambda_q2, lambda_k2, subln_g, w_hy_out, w_da_out, w_o):
    b, seq, d = x.shape
    lctx = ctx.shape[1]
    depth = w_ada.shape[0]
    d_hy = hy_bias.shape[1]
    assert depth == 1 and b % 2 == 0 and b < 16
    ctx_row = b
    tm = min(512, seq)
    tq = min(256, seq)
    tk = min(1536, seq)
    ct = LANES

    rope = _rope_tables(seq)
    f1_data, f1_real, f1_inv, dft_g, dft_h = _dft_tables(seq)
    c16 = jnp.zeros((16, d), F32).at[:b].set(c).at[ctx_row].set(c_ctx)
    latent_row = lambda bi: bi
    context_row = lambda bi: ctx_row
    modes = ("plain",) * (3 * d_hy // d) + ("rope_q", "rope_k", "plain", "sigmoid", "sigmoid")
    kv_chunk0 = 3 * d_hy // d + 1

    xs, cs = x, ctx
    for l in range(depth):
        lam_init = 0.8 - 0.6 * math.exp(-0.3 * l)
        mod = _ada(c16, w_ada[l], b_ada[l]).reshape(16, N_MOD, d)
        g = norm_g[l]
        wfi = w_ff_in[l].astype(BF16)
        wfo = w_ff_out[l].astype(BF16)
        wi = w_in[l].astype(BF16)
        filt = (filt_w1[l], filt_b1[l], filt_w2[l], filt_b2[l], filt_w3[l], filt_b3[l],
                filt_w4[l], filt_freq[l])
        lam_rows = jnp.stack([lambda_q1[l], lambda_k1[l], lambda_q2[l], lambda_k2[l]])

        xs = _ffn(xs, mod, g, wfi[0], wfo[0], mrow=0, grow=0, mod_row_of_batch=latent_row, tm=tm)
        cs = _ffn(cs, mod, g, wfi[0], wfo[0], mrow=0, grow=0, mod_row_of_batch=context_row,
                  tm=min(tm, lctx))

        p = _proj(xs, mod, g, wi, modes=modes, col_chunk0=0, mrow=3, grow=2,
                  mod_row_of_batch=latent_row, tm=tm, rope=rope)
        pc = _proj(cs, mod, g, wi, modes=("plain", "plain"), col_chunk0=kv_chunk0, mrow=3, grow=2,
                   mod_row_of_batch=context_row, tm=min(tm, lctx))

        kf = _hyena_filter_spectrum(seq, d_hy, filt, (f1_real, dft_g), ct)
        yh = _hyena(p, hy_conv_w[l], hy_conv_b[l], hy_bias[l], kf, (f1_data, f1_inv, dft_g, dft_h), d_hy, ct)
        od = _attn(p, pc, lam_rows, subln_g[l], lam_init, d_hy, tq, tk)
        xs = _merge(xs, yh, od, p, mod, g, w_hy_out[l].astype(BF16), w_da_out[l].astype(BF16),
                    w_o[l].astype(BF16), mrow=5, grow=3, gate_chunk0=kv_chunk0 + 2, tm=tm)
        xs = _ffn(xs, mod, g, wfi[1], wfo[1], mrow=6, grow=4, mod_row_of_batch=latent_row, tm=tm)
    return xs
```

```python
import functools
import math

import jax
import jax.numpy as jnp
from jax import lax
from jax.experimental import pallas as pl
from jax.experimental.pallas import tpu as pltpu

F32 = jnp.float32
BF16 = jnp.bfloat16
HIGHEST = lax.Precision.HIGHEST

N_HEADS = 8
HEAD_DIM = 64
GRID_W = 64
ROPE_THETA = 10000.0
SHORT_CONV = 3
FILTER_EMB = 33
DECAY_TARGET = 1e-2
FAST_DECAY_PCT = 0.3
SLOW_DECAY_PCT = 1.5
EPS = 1e-6
N_MOD = 9
Q_SCALE = HEAD_DIM ** -0.5 * math.log2(math.e)

LANES = 128
SUBLANES = 8
MXU_DIM = 256
DFT_N2 = 128
DFT_GROUP = 16
STAGE2_GROUP = 8
VMEM_LIMIT = 56 * 1024 * 1024


def _params(sem, vmem=VMEM_LIMIT):
    return pltpu.CompilerParams(dimension_semantics=sem, vmem_limit_bytes=vmem)


def _rms(x, g):
    return x * lax.rsqrt(jnp.mean(x * x, axis=-1, keepdims=True) + EPS) * g


def _dot(a, b, **kw):
    return jnp.dot(a, b, preferred_element_type=F32, **kw)


def _norm_modulate(x, mod_ref, g_ref, mrow, grow):
    shift = mod_ref[0, mrow:mrow + 1, :]
    scale = mod_ref[0, mrow + 1:mrow + 2, :]
    return (_rms(x, g_ref[grow:grow + 1, :]) * (1.0 + scale) + shift).astype(BF16)


def _ada_kernel(c_ref, w_ref, b_ref, o_ref):
    c = c_ref[...]
    a = c * jax.nn.sigmoid(c)
    o_ref[...] = _dot(a, w_ref[...], precision=HIGHEST) + b_ref[...]


def _ada(c16, w_ada, b_ada):
    d = c16.shape[1]
    n = w_ada.shape[1]
    return pl.pallas_call(
        _ada_kernel,
        grid=(n // d,),
        in_specs=[pl.BlockSpec((16, d), lambda j: (0, 0)),
                  pl.BlockSpec((d, d), lambda j: (0, j)),
                  pl.BlockSpec((1, d), lambda j: (0, j))],
        out_specs=pl.BlockSpec((16, d), lambda j: (0, j)),
        out_shape=jax.ShapeDtypeStruct((16, n), F32),
        compiler_params=_params(("arbitrary",)),
        name="ada",
    )(c16, w_ada, b_ada.reshape(1, n))


def _ffn_kernel(mrow, grow, bounds, s_ref, mod_ref, g_ref, wi_ref, wo_ref, o_ref):
    dff = wo_ref.shape[0]
    x = s_ref[0]
    xn = _norm_modulate(x, mod_ref, g_ref, mrow, grow)
    acc = None
    for lo, hi in zip(bounds[:-1], bounds[1:]):
        hg = _dot(xn, wi_ref[:, lo:hi])
        hu = _dot(xn, wi_ref[:, dff + lo:dff + hi])
        h = (hg * jax.nn.sigmoid(hg) * hu).astype(BF16)
        part = _dot(h, wo_ref[lo:hi, :])
        acc = part if acc is None else acc + part
    gate = mod_ref[0, mrow + 2:mrow + 3, :]
    o_ref[0] = x + 0.5 * gate * _rms(acc, g_ref[grow + 1:grow + 2, :])


def _ffn(s, mod, g, w_in, w_out, *, half, mrow, grow, mod_row_of_batch, tm):
    b, t, d = s.shape
    dff = w_out.shape[1]
    split = (dff // MXU_DIM + 1) // 2 * MXU_DIM
    bounds = (0, split, dff)
    once = pl.Buffered(1)
    return pl.pallas_call(
        functools.partial(_ffn_kernel, mrow, grow, bounds),
        grid=(b, t // tm),
        in_specs=[pl.BlockSpec((1, tm, d), lambda bi, i: (bi, i, 0)),
                  pl.BlockSpec((1, N_MOD, d), lambda bi, i: (mod_row_of_batch(bi), 0, 0)),
                  pl.BlockSpec(g.shape, lambda bi, i: (0, 0)),
                  pl.BlockSpec((None,) + w_in.shape[1:], lambda bi, i: (half, 0, 0), pipeline_mode=once),
                  pl.BlockSpec((None,) + w_out.shape[1:], lambda bi, i: (half, 0, 0), pipeline_mode=once)],
        out_specs=pl.BlockSpec((1, tm, d), lambda bi, i: (bi, i, 0)),
        out_shape=jax.ShapeDtypeStruct((b, t, d), F32),
        compiler_params=_params(("parallel", "parallel")),
        name="ffn",
    )(s, mod, g, w_in, w_out)


def _rope(acc, cos, sa, sb):
    cols = []
    for k in range(acc.shape[1] // LANES):
        blk = acc[:, k * LANES:(k + 1) * LANES]
        cols.append(blk * cos + pltpu.roll(blk, 16, 1) * sa + pltpu.roll(blk, LANES - 16, 1) * sb)
    return jnp.concatenate(cols, axis=1)


def _proj_kernel(modes, mrow, grow, x_ref, mod_ref, g_ref, w_ref, *rest):
    o_ref = rest[-1]
    d = x_ref.shape[2]
    xn = _norm_modulate(x_ref[0], mod_ref, g_ref, mrow, grow)
    for jj, mode in enumerate(modes):
        acc = _dot(xn, w_ref[:, jj * d:(jj + 1) * d])
        if mode == "sigmoid":
            acc = jax.nn.sigmoid(acc)
        elif mode in ("rope_q", "rope_k"):
            cos_ref, sa_ref, sb_ref = rest[:3]
            acc = _rope(acc, cos_ref[...], sa_ref[...], sb_ref[...])
            if mode == "rope_q":
                acc = acc * Q_SCALE
        o_ref[0, :, jj * d:(jj + 1) * d] = acc.astype(BF16)


def _proj(x, mod, g, w, *, modes, col_chunk0, mrow, grow, mod_row_of_batch, tm, rope=None):
    b, t, d = x.shape
    nch = len(modes)
    assert col_chunk0 % nch == 0
    in_specs = [pl.BlockSpec((1, tm, d), lambda bi, i: (bi, i, 0)),
                pl.BlockSpec((1, N_MOD, d), lambda bi, i: (mod_row_of_batch(bi), 0, 0)),
                pl.BlockSpec(g.shape, lambda bi, i: (0, 0)),
                pl.BlockSpec((d, nch * d), lambda bi, i: (0, col_chunk0 // nch),
                             pipeline_mode=pl.Buffered(1))]
    args = [x, mod, g, w]
    if rope is not None:
        in_specs += [pl.BlockSpec((tm, LANES), lambda bi, i: (i, 0))] * 3
        args += list(rope)
    return pl.pallas_call(
        functools.partial(_proj_kernel, modes, mrow, grow),
        grid=(b, t // tm),
        in_specs=in_specs,
        out_specs=pl.BlockSpec((1, tm, nch * d), lambda bi, i: (bi, i, 0)),
        out_shape=jax.ShapeDtypeStruct((b, t, nch * d), BF16),
        compiler_params=_params(("parallel", "parallel")),
        name="proj",
    )(*args)


def _rope_tables(seq):
    pos = jnp.arange(seq)
    r = (pos // GRID_W).astype(F32)[:, None]
    col = (pos % GRID_W).astype(F32)[:, None]
    half = HEAD_DIM // 2
    inv = ROPE_THETA ** (-jnp.arange(0, half, 2, dtype=F32) / half)
    cr, sr = jnp.cos(r * inv), jnp.sin(r * inv)
    cc, sc = jnp.cos(col * inv), jnp.sin(col * inv)
    z = jnp.zeros_like(sr)
    cos = jnp.concatenate([cr, cr, cc, cc], axis=1)
    sa = jnp.concatenate([z, sr, z, sc], axis=1)
    sb = jnp.concatenate([-sr, z, -sc, z], axis=1)
    rep = LANES // HEAD_DIM
    return tuple(jnp.tile(t, (1, rep)) for t in (cos, sa, sb))


def _dft_tables(seq):
    n = 2 * seq
    n2 = DFT_N2
    n1 = n // n2
    k1h = n1 // 2

    def cs(num, den):
        ang = (2.0 * math.pi / den) * (num % den).astype(F32)
        return jnp.cos(ang), jnp.sin(ang)

    i1 = jnp.arange(n1)
    c, s = cs(i1[:, None] * i1[None, :], n1)
    f1_data = jnp.concatenate([jnp.concatenate([c[:, :k1h], s[:, :k1h]], 1),
                               jnp.concatenate([-s[:, :k1h], c[:, :k1h]], 1)], 0)
    f1_real = jnp.concatenate([c, -s], 0)
    ci, si = c[:k1h, :] / n, s[:k1h, :] / n
    f1_inv = jnp.concatenate([jnp.concatenate([ci, -si], 1),
                              jnp.concatenate([si, ci], 1)], 0)
    i2 = jnp.arange(n2)
    num = i2[None, None, :] * (i1[:, None, None] + n1 * i2[None, :, None])
    c2, s2 = cs(num, n)
    g = jnp.concatenate([jnp.concatenate([c2, s2], 2),
                         jnp.concatenate([-s2, c2], 2)], 1)
    h = jnp.swapaxes(g, 1, 2)
    return (f1_data.astype(BF16), f1_real.astype(BF16), f1_inv.astype(BF16),
            g.astype(BF16), h.astype(BF16))


def _lane_blocks_to_rows(x, group):
    ct = x.shape[1] // group
    return jnp.swapaxes(jnp.stack([x[:, j * ct:(j + 1) * ct] for j in range(group)]), 0, 1)


def _rows_to_lane_blocks(x3):
    xt = jnp.swapaxes(x3, 0, 1)
    return jnp.concatenate([xt[j] for j in range(xt.shape[0])], axis=1)


def _dft_stage1(x_ref, f1, t_ref, group):
    def body(i, carry):
        rows = pl.ds(pl.multiple_of(i * group, group), group)
        z = _rows_to_lane_blocks(x_ref[:, rows, :]).astype(BF16)
        t_ref[:, rows, :] = _lane_blocks_to_rows(_dot(f1, z), group).astype(t_ref.dtype)
        return carry

    lax.fori_loop(0, DFT_N2 // group, body, 0)


def _stage2_operand(t_ref, q, n1):
    return jnp.concatenate([t_ref[q], t_ref[n1 + q]], axis=0)


def _filt_mlp_kernel(z_ref, w1_ref, b1_ref, w2_ref, b2_ref, w3_ref, b3_ref, fr_ref, o_ref):
    fr = fr_ref[...]
    h = jnp.sin(fr * (_dot(z_ref[...], w1_ref[...], precision=HIGHEST) + b1_ref[...]))
    h = jnp.sin(fr * (_dot(h, w2_ref[...], precision=HIGHEST) + b2_ref[...]))
    o_ref[...] = jnp.sin(fr * (_dot(h, w3_ref[...], precision=HIGHEST) + b3_ref[...]))


def _filt_spec_kernel(seq, n1, group1, group2, h_ref, w4f_ref, w4b_ref, t_ref, dl_ref, f1_ref, g_ref,
                      o_ref, k_ref, r_ref):
    n2 = DFT_N2
    n = 2 * seq
    ct = o_ref.shape[2]
    row = lax.broadcasted_iota(jnp.int32, (n, ct), 0)
    hk = jnp.concatenate([_dot(h_ref[0:seq, :], w4f_ref[...], precision=HIGHEST),
                          _dot(h_ref[seq:n, :], w4b_ref[...], precision=HIGHEST)], axis=0)
    kraw = jnp.where(row == seq, 0.0, hk * jnp.exp(-t_ref[...] * dl_ref[...]))
    k_ref[...] = (kraw / (jnp.sum(jnp.abs(kraw), axis=0, keepdims=True) + EPS)).reshape(n1, n2, ct)

    _dft_stage1(k_ref, f1_ref[...], r_ref, group1)

    def stage2(i, carry):
        q0 = i * group2
        ops = [_stage2_operand(r_ref, q0 + j, n1) for j in range(group2)]
        for j in range(group2):
            o_ref[q0 + j] = _dot(g_ref[q0 + j], ops[j]).astype(o_ref.dtype)
        return carry

    lax.fori_loop(0, n1 // group2, stage2, 0)


def _hyena_filter_spectrum(seq, d_hy, filt, tables, ct):
    w1, b1, w2, b2, w3, b3, w4, freq = filt
    f1_real, g = tables
    n = 2 * seq
    n2 = DFT_N2
    n1 = n // n2
    order = w1.shape[1]
    pad = LANES - order
    bands = (FILTER_EMB - 1) // 2
    t = jnp.linspace(0.0, 1.0, seq, dtype=F32)[:, None]
    w = 2.0 * math.pi * jnp.arange(seq, dtype=F32)[:, None] / seq
    f = jnp.linspace(1e-4, bands - 1, bands, dtype=F32)[None, :]
    z = jnp.concatenate([t, jnp.cos(f * w), -jnp.sin(f * w)], axis=-1)
    def circular(a):
        return jnp.concatenate([a, a[:1], a[:0:-1]], axis=0)

    zp = jnp.pad(z, ((0, 0), (0, LANES - FILTER_EMB)))
    t_ext = circular(t)
    deltas = jnp.abs(jnp.linspace(math.log(DECAY_TARGET) / SLOW_DECAY_PCT,
                                  math.log(DECAY_TARGET) / FAST_DECAY_PCT, d_hy, dtype=F32))[None, :]

    def padc(a):
        return jnp.pad(a, ((0, 0), (0, pad)))

    w1p = jnp.pad(w1, ((0, LANES - FILTER_EMB), (0, pad)))
    w2p = jnp.pad(w2, ((0, pad), (0, pad)))
    w3p = jnp.pad(w3, ((0, pad), (0, pad)))
    w4p = jnp.pad(w4, ((0, pad), (0, 0)))
    h3 = pl.pallas_call(
        _filt_mlp_kernel,
        out_shape=jax.ShapeDtypeStruct((seq, LANES), F32),
        compiler_params=pltpu.CompilerParams(vmem_limit_bytes=VMEM_LIMIT),
        name="filt_mlp",
    )(zp, w1p, padc(b1[None]), w2p, padc(b2[None]), w3p, padc(b3[None]), padc(freq[None]))
    h3 = circular(h3)

    nct = d_hy // ct
    return pl.pallas_call(
        functools.partial(_filt_spec_kernel, seq, n1, DFT_GROUP, min(STAGE2_GROUP, n1)),
        grid=(nct,),
        in_specs=[pl.BlockSpec((n, LANES), lambda j: (0, 0)),
                  pl.BlockSpec((LANES, ct), lambda j: (0, j)),
                  pl.BlockSpec((LANES, ct), lambda j: (0, nct + j)),
                  pl.BlockSpec((n, 1), lambda j: (0, 0)),
                  pl.BlockSpec((1, ct), lambda j: (0, j)),
                  pl.BlockSpec(f1_real.shape, lambda j: (0, 0)),
                  pl.BlockSpec(g.shape, lambda j: (0, 0, 0))],
        out_specs=pl.BlockSpec((n1, 2 * n2, ct), lambda j: (0, 0, j)),
        out_shape=jax.ShapeDtypeStruct((n1, 2 * n2, d_hy), BF16),
        scratch_shapes=[pltpu.VMEM((n1, n2, ct), F32), pltpu.VMEM((2 * n1, n2, ct), BF16)],
        compiler_params=_params(("arbitrary",)),
        name="filt_spec",
    )(h3, w4p, w4p, t_ext, deltas, f1_real, g)


def _conv3(z, w, b):
    seq = z.shape[0]
    row = lax.broadcasted_iota(jnp.int32, z.shape, 0)
    zm = jnp.where(row == 0, 0.0, pltpu.roll(z, 1, 0))
    zp = jnp.where(row == seq - 1, 0.0, pltpu.roll(z, seq - 1, 0))
    return zm * w[0:1, :] + z * w[1:2, :] + zp * w[2:3, :] + b


def _hyena_kernel(seq, n1, group1, group2, x0_ref, x1_ref, v_ref, cw0_ref, cw1_ref, cwv_ref,
                  cb0_ref, cb1_ref, cbv_ref, bias_ref, f1_ref, f1i_ref, g_ref, h_ref, kf_ref,
                  o_ref, u_ref, r_ref):
    n2 = DFT_N2
    k1h = n1 // 2
    ct = o_ref.shape[2]

    for bb in range(2):
        x1 = _conv3(x1_ref[bb].astype(F32), cw1_ref[...], cb1_ref[...])
        v = _conv3(v_ref[bb].astype(F32), cwv_ref[...], cbv_ref[...])
        u_ref[bb * k1h:(bb + 1) * k1h] = (v * x1).reshape(k1h, n2, ct)

    _dft_stage1(u_ref, f1_ref[...], r_ref, group1)

    def stage2(i, carry):
        q0 = i * group2
        ops = [_stage2_operand(r_ref, q0 + j, n1) for j in range(group2)]
        xs = [_dot(g_ref[q0 + j], ops[j]) for j in range(group2)]
        ys = []
        for j in range(group2):
            kf = kf_ref[q0 + j].astype(F32)
            xr, xi, kr, ki = xs[j][:n2], xs[j][n2:], kf[:n2], kf[n2:]
            ys.append(jnp.concatenate([xr * kr - xi * ki, xr * ki + xi * kr], axis=0).astype(BF16))
        outs = [_dot(h_ref[q0 + j], ys[j]) for j in range(group2)]
        for j in range(group2):
            r_ref[q0 + j] = outs[j][:n2].astype(r_ref.dtype)
            r_ref[n1 + q0 + j] = outs[j][n2:].astype(r_ref.dtype)
        return carry

    lax.fori_loop(0, n1 // group2, stage2, 0)

    f1i = f1i_ref[...]
    bias = bias_ref[...]

    def stage3(i, carry):
        rows = pl.ds(pl.multiple_of(i * group1, group1), group1)
        b = _rows_to_lane_blocks(r_ref[:, rows, :].astype(F32)).astype(BF16)
        y = _lane_blocks_to_rows(_dot(f1i, b), group1)
        u_ref[:, rows, :] = y + u_ref[:, rows, :] * bias
        return carry

    lax.fori_loop(0, n2 // group1, stage3, 0)

    for bb in range(2):
        x0 = _conv3(x0_ref[bb].astype(F32), cw0_ref[...], cb0_ref[...])
        o_ref[bb] = (u_ref[bb * k1h:(bb + 1) * k1h].reshape(seq, ct) * x0).astype(BF16)


def _hyena(p, conv_w, conv_b, hy_bias, kf, tables, d_hy, ct):
    b, seq, _ = p.shape
    f1_data, f1_inv, g, h = tables
    n2 = DFT_N2
    n1 = 2 * seq // n2
    nct = d_hy // ct
    once = pl.Buffered(1)
    zspec = lambda off: pl.BlockSpec((2, seq, ct), lambda j, q: (q, 0, off * nct + j))
    wspec = lambda off: pl.BlockSpec((SHORT_CONV, ct), lambda j, q: (0, off * nct + j))
    bspec = lambda off: pl.BlockSpec((1, ct), lambda j, q: (0, off * nct + j))
    cb = conv_b.reshape(1, -1)
    return pl.pallas_call(
        functools.partial(_hyena_kernel, seq, n1, DFT_GROUP, min(STAGE2_GROUP, n1)),
        grid=(nct, b // 2),
        in_specs=[zspec(0), zspec(1), zspec(2), wspec(0), wspec(1), wspec(2), bspec(0), bspec(1), bspec(2),
                  pl.BlockSpec((1, ct), lambda j, q: (0, j)),
                  pl.BlockSpec(f1_data.shape, lambda j, q: (0, 0)),
                  pl.BlockSpec(f1_inv.shape, lambda j, q: (0, 0)),
                  pl.BlockSpec(g.shape, lambda j, q: (0, 0, 0), pipeline_mode=once),
                  pl.BlockSpec(h.shape, lambda j, q: (0, 0, 0), pipeline_mode=once),
                  pl.BlockSpec((n1, 2 * n2, ct), lambda j, q: (0, 0, j), pipeline_mode=once)],
        out_specs=pl.BlockSpec((2, seq, ct), lambda j, q: (q, 0, j)),
        out_shape=jax.ShapeDtypeStruct((b, seq, d_hy), BF16),
        scratch_shapes=[pltpu.VMEM((n1, n2, ct), F32), pltpu.VMEM((2 * n1, n2, ct), BF16)],
        compiler_params=_params(("arbitrary", "arbitrary")),
        name="hyena",
    )(p, p, p, conv_w, conv_w, conv_w, cb, cb, cb, hy_bias.reshape(1, -1), f1_data, f1_inv, g, h, kf)


def _attn_kernel(lam_init, chunks, tq, q_ref, k_ref, v_ref, kc_ref, vc_ref, lam_ref, g_ref, o_ref, vx_ref):
    seq = k_ref.shape[1]
    lctx = kc_ref.shape[1]
    hw = q_ref.shape[2]
    nt = (((1,), (1,)), ((), ()))

    vx_ref[0:seq, 0:hw] = v_ref[0]
    vx_ref[seq:seq + lctx, 0:hw] = vc_ref[0]
    vx_ref[:, hw:2 * hw] = jnp.ones((seq + lctx, hw), BF16)

    lp = lam_ref[...]
    lam = (jnp.exp(jnp.sum(lp[0:1] * lp[1:2], axis=-1, keepdims=True))
           - jnp.exp(jnp.sum(lp[2:3] * lp[3:4], axis=-1, keepdims=True)) + lam_init)

    def q_tile(i, carry):
        rows = pl.ds(pl.multiple_of(i * tq, tq), tq)
        q = q_ref[0, rows, :]
        lane = lax.broadcasted_iota(jnp.int32, q.shape, 1)
        zero = jnp.zeros_like(q)
        qq = jnp.concatenate([jnp.where(lane < HEAD_DIM, q, zero), jnp.where(lane >= HEAD_DIM, q, zero)],
                             axis=0)

        def scores(lo, hi):
            parts = []
            if lo < seq:
                parts.append(lax.dot_general(qq, k_ref[0, lo:min(hi, seq), :], nt, preferred_element_type=F32))
            if hi > seq:
                parts.append(lax.dot_general(qq, kc_ref[0, max(lo, seq) - seq:hi - seq, :], nt,
                                             preferred_element_type=F32))
            return parts[0] if len(parts) == 1 else jnp.concatenate(parts, axis=1)

        m = acc = None
        s_next = scores(*chunks[0])
        for ci, (lo, hi) in enumerate(chunks):
            s = s_next
            if ci + 1 < len(chunks):
                s_next = scores(*chunks[ci + 1])
            mc = jnp.max(s, axis=-1, keepdims=True)
            m_new = mc if m is None else jnp.maximum(m, mc)
            pv = _dot(jnp.exp2(s - m_new).astype(BF16), vx_ref[lo:hi, :])
            acc = pv if acc is None else acc * jnp.exp2(m - m_new) + pv
            m = m_new

        o12 = acc[:, :hw] / acc[:, hw:]
        o = o12[:tq] - lam * o12[tq:]
        o_ref[0, rows, :] = (_rms(o, g_ref[...]) * (1.0 - lam_init)).astype(BF16)
        return carry

    lax.fori_loop(0, seq // tq, q_tile, 0, unroll=4)


def _attn(p, pc, lam_rows, subln_g, lam_init, d_hy, tq, tk):
    b, seq, _ = p.shape
    lctx = pc.shape[1]
    hw = 2 * HEAD_DIM
    c0 = 3 * d_hy // hw
    nh = N_HEADS
    edges = list(range(0, seq, tk)) + [seq + lctx]
    chunks = tuple(zip(edges[:-1], edges[1:]))
    return pl.pallas_call(
        functools.partial(_attn_kernel, lam_init, chunks, tq),
        grid=(b, nh),
        in_specs=[pl.BlockSpec((1, seq, hw), lambda bi, hh: (bi, 0, c0 + hh)),
                  pl.BlockSpec((1, seq, hw), lambda bi, hh: (bi, 0, c0 + nh + hh)),
                  pl.BlockSpec((1, seq, hw), lambda bi, hh: (bi, 0, c0 + 2 * nh + hh)),
                  pl.BlockSpec((1, lctx, hw), lambda bi, hh: (bi, 0, hh)),
                  pl.BlockSpec((1, lctx, hw), lambda bi, hh: (bi, 0, nh + hh)),
                  pl.BlockSpec(lam_rows.shape, lambda bi, hh: (0, 0)),
                  pl.BlockSpec((1, hw), lambda bi, hh: (0, 0))],
        out_specs=pl.BlockSpec((1, seq, hw), lambda bi, hh: (bi, 0, hh)),
        out_shape=jax.ShapeDtypeStruct((b, seq, nh * hw), BF16),
        scratch_shapes=[pltpu.VMEM((seq + lctx, 2 * hw), BF16)],
        compiler_params=_params(("parallel", "parallel")),
        name="attn",
    )(p, p, p, pc, pc, lam_rows, subln_g.reshape(1, hw))


def _merge_kernel(mrow, grow, s_ref, yh_ref, od_ref, gh_ref, gd_ref, mod_ref, g_ref,
                  why_ref, wda_ref, wo_ref, o_ref):
    y_hy = _dot(yh_ref[0], why_ref[...])
    y_da = _dot(od_ref[0], wda_ref[...])
    mix = gh_ref[0].astype(F32) * y_hy + gd_ref[0].astype(F32) * y_da
    y = _dot(mix.astype(BF16), wo_ref[...])
    o_ref[0] = s_ref[0] + mod_ref[0, mrow:mrow + 1, :] * _rms(y, g_ref[grow:grow + 1, :])


def _merge(s, yh, od, p, mod, g, w_hy_out, w_da_out, w_o, *, mrow, grow, gate_chunk0, tm):
    b, t, d = s.shape
    tok = lambda c: pl.BlockSpec((1, tm, d), lambda bi, i: (bi, i, c))
    wfull = lambda w: pl.BlockSpec(w.shape, lambda bi, i: (0, 0))
    return pl.pallas_call(
        functools.partial(_merge_kernel, mrow, grow),
        grid=(b, t // tm),
        in_specs=[tok(0), tok(0), tok(0), tok(gate_chunk0), tok(gate_chunk0 + 1),
                  pl.BlockSpec((1, N_MOD, d), lambda bi, i: (bi, 0, 0)),
                  pl.BlockSpec(g.shape, lambda bi, i: (0, 0)),
                  wfull(w_hy_out), wfull(w_da_out), wfull(w_o)],
        out_specs=tok(0),
        out_shape=jax.ShapeDtypeStruct((b, t, d), F32),
        compiler_params=_params(("parallel", "parallel")),
        name="merge",
    )(s, yh, od, p, p, mod, g, w_hy_out, w_da_out, w_o)


def kernel(x, c, ctx, c_ctx, w_ada, b_ada, norm_g, w_ff_in, w_ff_out, w_in, hy_conv_w, hy_conv_b, filt_w1, filt_b1, filt_w2, filt_b2, filt_w3, filt_b3, filt_w4, filt_freq, hy_bias, lambda_q1, lambda_k1, lambda_q2, lambda_k2, subln_g, w_hy_out, w_da_out, w_o):
    b, seq, d = x.shape
    lctx = ctx.shape[1]
    depth = w_ada.shape[0]
    d_hy = hy_bias.shape[1]
    assert depth == 1 and b % 2 == 0 and b < 16
    ctx_row = b
    tm = min(512, seq)
    tq = min(256, seq)
    tk = min(1536, seq)
    ct = LANES

    rope = _rope_tables(seq)
    f1_data, f1_real, f1_inv, dft_g, dft_h = _dft_tables(seq)
    c16 = jnp.zeros((16, d), F32).at[:b].set(c).at[ctx_row].set(c_ctx)
    latent_row = lambda bi: bi
    context_row = lambda bi: ctx_row
    modes = ("plain",) * (3 * d_hy // d) + ("rope_q", "rope_k", "plain", "sigmoid", "sigmoid")
    kv_chunk0 = 3 * d_hy // d + 1

    xs, cs = x, ctx
    for l in range(depth):
        lam_init = 0.8 - 0.6 * math.exp(-0.3 * l)
        mod = _ada(c16, w_ada[l], b_ada[l]).reshape(16, N_MOD, d)
        g = norm_g[l]
        wfi = w_ff_in[l].astype(BF16)
        wfo = w_ff_out[l].astype(BF16)
        wi = w_in[l].astype(BF16)
        filt = (filt_w1[l], filt_b1[l], filt_w2[l], filt_b2[l], filt_w3[l], filt_b3[l],
                filt_w4[l], filt_freq[l])
        lam_rows = jnp.stack([lambda_q1[l], lambda_k1[l], lambda_q2[l], lambda_k2[l]])

        xs = _ffn(xs, mod, g, wfi, wfo, half=0, mrow=0, grow=0, mod_row_of_batch=latent_row, tm=tm)
        cs = _ffn(cs, mod, g, wfi, wfo, half=0, mrow=0, grow=0, mod_row_of_batch=context_row,
                  tm=min(tm, lctx))

        p = _proj(xs, mod, g, wi, modes=modes, col_chunk0=0, mrow=3, grow=2,
                  mod_row_of_batch=latent_row, tm=tm, rope=rope)
        pc = _proj(cs, mod, g, wi, modes=("plain", "plain"), col_chunk0=kv_chunk0, mrow=3, grow=2,
                   mod_row_of_batch=context_row, tm=min(tm, lctx))

        kf = _hyena_filter_spectrum(seq, d_hy, filt, (f1_real, dft_g), ct)
        yh = _hyena(p, hy_conv_w[l], hy_conv_b[l], hy_bias[l], kf, (f1_data, f1_inv, dft_g, dft_h), d_hy, ct)
        od = _attn(p, pc, lam_rows, subln_g[l], lam_init, d_hy, tq, tk)
        xs = _merge(xs, yh, od, p, mod, g, w_hy_out[l].astype(BF16), w_da_out[l].astype(BF16),
                    w_o[l].astype(BF16), mrow=5, grow=3, gate_chunk0=kv_chunk0 + 2, tm=tm)
        xs = _ffn(xs, mod, g, wfi, wfo, half=1, mrow=6, grow=4, mod_row_of_batch=latent_row, tm=tm)
    return xs
```

```python
import functools
import math

import jax
import jax.numpy as jnp
from jax import lax
from jax.experimental import pallas as pl
from jax.experimental.pallas import tpu as pltpu

F32 = jnp.float32
BF16 = jnp.bfloat16
HIGHEST = lax.Precision.HIGHEST

N_HEADS = 8
HEAD_DIM = 64
GRID_W = 64
ROPE_THETA = 10000.0
FILTER_EMB = 33
DECAY_TARGET = 1e-2
FAST_DECAY_PCT = 0.3
SLOW_DECAY_PCT = 1.5
EPS = 1e-6
N_MOD = 9
Q_SCALE = HEAD_DIM ** -0.5 * math.log2(math.e)

LANES = 128
MXU_DIM = 256
DFT_N2 = 128
HALO = 16
DFT_GROUP = 16
STAGE2_GROUP = 8
VMEM_LIMIT = 56 * 1024 * 1024


def _params(sem, vmem=VMEM_LIMIT):
    return pltpu.CompilerParams(dimension_semantics=sem, vmem_limit_bytes=vmem)


def _rms(x, g):
    return x * lax.rsqrt(jnp.mean(x * x, axis=-1, keepdims=True) + EPS) * g


def _dot(a, b, **kw):
    return jnp.dot(a, b, preferred_element_type=F32, **kw)


def _norm_modulate(x, mod_ref, g_ref, mrow, grow, keep=None):
    shift = mod_ref[0, mrow:mrow + 1, :]
    scale = mod_ref[0, mrow + 1:mrow + 2, :]
    y = _rms(x, g_ref[grow:grow + 1, :]) * (1.0 + scale) + shift
    return (y if keep is None else y * keep).astype(BF16)


def _ada_kernel(c_ref, w_ref, b_ref, o_ref):
    c = c_ref[...]
    a = c * jax.nn.sigmoid(c)
    o_ref[...] = _dot(a, w_ref[...], precision=HIGHEST) + b_ref[...]


def _ada(c16, w_ada, b_ada):
    d = c16.shape[1]
    n = w_ada.shape[1]
    return pl.pallas_call(
        _ada_kernel,
        grid=(n // d,),
        in_specs=[pl.BlockSpec((16, d), lambda j: (0, 0)),
                  pl.BlockSpec((d, d), lambda j: (0, j)),
                  pl.BlockSpec((1, d), lambda j: (0, j))],
        out_specs=pl.BlockSpec((16, d), lambda j: (0, j)),
        out_shape=jax.ShapeDtypeStruct((16, n), F32),
        compiler_params=_params(("arbitrary",)),
        name="ada",
    )(c16, w_ada, b_ada.reshape(1, n))


def _ffn_kernel(mrow, grow, bounds, s_ref, mod_ref, g_ref, wi_ref, wo_ref, o_ref):
    dff = wo_ref.shape[0]
    x = s_ref[0]
    xn = _norm_modulate(x, mod_ref, g_ref, mrow, grow)
    acc = None
    for lo, hi in zip(bounds[:-1], bounds[1:]):
        hg = _dot(xn, wi_ref[:, lo:hi])
        hu = _dot(xn, wi_ref[:, dff + lo:dff + hi])
        h = (hg * jax.nn.sigmoid(hg) * hu).astype(BF16)
        part = _dot(h, wo_ref[lo:hi, :])
        acc = part if acc is None else acc + part
    gate = mod_ref[0, mrow + 2:mrow + 3, :]
    o_ref[0] = x + 0.5 * gate * _rms(acc, g_ref[grow + 1:grow + 2, :])


def _ffn(s, mod, g, w_in, w_out, *, half, mrow, grow, mod_row_of_batch, tm):
    b, t, d = s.shape
    dff = w_out.shape[1]
    split = (dff // MXU_DIM + 1) // 2 * MXU_DIM
    bounds = (0, split, dff)
    once = pl.Buffered(1)
    return pl.pallas_call(
        functools.partial(_ffn_kernel, mrow, grow, bounds),
        grid=(b, t // tm),
        in_specs=[pl.BlockSpec((1, tm, d), lambda bi, i: (bi, i, 0)),
                  pl.BlockSpec((1, N_MOD, d), lambda bi, i: (mod_row_of_batch(bi), 0, 0)),
                  pl.BlockSpec(g.shape, lambda bi, i: (0, 0)),
                  pl.BlockSpec((None,) + w_in.shape[1:], lambda bi, i: (half, 0, 0), pipeline_mode=once),
                  pl.BlockSpec((None,) + w_out.shape[1:], lambda bi, i: (half, 0, 0), pipeline_mode=once)],
        out_specs=pl.BlockSpec((1, tm, d), lambda bi, i: (bi, i, 0)),
        out_shape=jax.ShapeDtypeStruct((b, t, d), F32),
        compiler_params=_params(("parallel", "parallel")),
        name="ffn",
    )(s, mod, g, w_in, w_out)


def _rope(acc, cos, sa, sb):
    cols = []
    for k in range(acc.shape[1] // LANES):
        blk = acc[:, k * LANES:(k + 1) * LANES]
        cols.append(blk * cos + pltpu.roll(blk, 16, 1) * sa + pltpu.roll(blk, LANES - 16, 1) * sb)
    return jnp.concatenate(cols, axis=1)


def _proj_kernel(modes, mrow, grow, x_ref, mod_ref, g_ref, w_ref, *rest):
    o_ref = rest[-1]
    d, tm = x_ref.shape[2], x_ref.shape[1]
    xn = _norm_modulate(x_ref[0], mod_ref, g_ref, mrow, grow)
    conv = any(m.startswith("conv") for m in modes)
    if conv:
        prev_ref, next_ref, cw_ref, cb_ref = rest[:4]
        rest = rest[4:]
        i = pl.program_id(1)
        keep_prev = (i > 0).astype(F32)
        keep_next = (i < pl.num_programs(1) - 1).astype(F32)
        xn_ext = jnp.concatenate(
            [_norm_modulate(prev_ref[0], mod_ref, g_ref, mrow, grow, keep_prev), xn,
             _norm_modulate(next_ref[0], mod_ref, g_ref, mrow, grow, keep_next)], axis=0)

        def conv_cols(lo):
            cols = slice(lo, lo + MXU_DIM)
            z = _dot(xn_ext, w_ref[:, cols])
            w3 = cw_ref[:, cols]
            y = pltpu.roll(z, 1, 0) * w3[0:1] + z * w3[1:2] + pltpu.roll(z, z.shape[0] - 1, 0) * w3[2:3]
            return y[HALO:HALO + tm] + cb_ref[:, cols]

        gate0, a0, b0 = (modes.index(m) * d for m in ("conv_gate", "conv_a", "conv_b"))

        def conv_block(lo):
            o_ref[0, :, lo:lo + MXU_DIM] = conv_cols(gate0 + lo).astype(BF16)
            o_ref[0, :, d + lo:d + lo + MXU_DIM] = (conv_cols(a0 + lo) * conv_cols(b0 + lo)).astype(BF16)

        conv_blocks = list(range(0, d, MXU_DIM))
    out_j = 2 if conv else 0
    for jj, mode in enumerate(modes):
        if mode.startswith("conv"):
            continue
        if conv and conv_blocks:
            conv_block(conv_blocks.pop(0))
        acc = _dot(xn, w_ref[:, jj * d:(jj + 1) * d])
        if mode == "sigmoid":
            acc = jax.nn.sigmoid(acc)
        elif mode in ("rope_q", "rope_k"):
            cos_ref, sa_ref, sb_ref = rest[:3]
            acc = _rope(acc, cos_ref[...], sa_ref[...], sb_ref[...])
            if mode == "rope_q":
                acc = acc * Q_SCALE
        o_ref[0, :, out_j * d:(out_j + 1) * d] = acc.astype(BF16)
        out_j += 1
    if conv:
        for lo in conv_blocks:
            conv_block(lo)


def _proj(x, mod, g, w, *, modes, col_chunk0, mrow, grow, mod_row_of_batch, tm, rope=None, conv=None):
    b, t, d = x.shape
    nch = len(modes)
    n_out = nch - sum(m == "conv_a" for m in modes)
    assert col_chunk0 % nch == 0
    in_specs = [pl.BlockSpec((1, tm, d), lambda bi, i: (bi, i, 0)),
                pl.BlockSpec((1, N_MOD, d), lambda bi, i: (mod_row_of_batch(bi), 0, 0)),
                pl.BlockSpec(g.shape, lambda bi, i: (0, 0)),
                pl.BlockSpec((d, nch * d), lambda bi, i: (0, col_chunk0 // nch),
                             pipeline_mode=pl.Buffered(1))]
    args = [x, mod, g, w]
    if conv is not None:
        per_tile = tm // HALO
        conv_w, conv_b = conv
        in_specs += [pl.BlockSpec((1, HALO, d), lambda bi, i: (bi, jnp.maximum(i * per_tile - 1, 0), 0)),
                     pl.BlockSpec((1, HALO, d),
                                  lambda bi, i: (bi, jnp.minimum((i + 1) * per_tile, t // HALO - 1), 0)),
                     pl.BlockSpec(conv_w.shape, lambda bi, i: (0, 0)),
                     pl.BlockSpec(conv_b.shape, lambda bi, i: (0, 0))]
        args += [x, x, conv_w, conv_b]
    if rope is not None:
        in_specs += [pl.BlockSpec((tm, LANES), lambda bi, i: (i, 0))] * 3
        args += list(rope)
    return pl.pallas_call(
        functools.partial(_proj_kernel, modes, mrow, grow),
        grid=(b, t // tm),
        in_specs=in_specs,
        out_specs=pl.BlockSpec((1, tm, n_out * d), lambda bi, i: (bi, i, 0)),
        out_shape=jax.ShapeDtypeStruct((b, t, n_out * d), BF16),
        compiler_params=_params(("parallel", "parallel")),
        name="proj",
    )(*args)


def _rope_tables(seq):
    pos = jnp.arange(seq)
    r = (pos // GRID_W).astype(F32)[:, None]
    col = (pos % GRID_W).astype(F32)[:, None]
    half = HEAD_DIM // 2
    inv = ROPE_THETA ** (-jnp.arange(0, half, 2, dtype=F32) / half)
    cr, sr = jnp.cos(r * inv), jnp.sin(r * inv)
    cc, sc = jnp.cos(col * inv), jnp.sin(col * inv)
    z = jnp.zeros_like(sr)
    cos = jnp.concatenate([cr, cr, cc, cc], axis=1)
    sa = jnp.concatenate([z, sr, z, sc], axis=1)
    sb = jnp.concatenate([-sr, z, -sc, z], axis=1)
    rep = LANES // HEAD_DIM
    return tuple(jnp.tile(t, (1, rep)) for t in (cos, sa, sb))


def _dft_tables(seq):
    n = 2 * seq
    n2 = DFT_N2
    n1 = n // n2
    k1h = n1 // 2

    def cs(num, den):
        ang = (2.0 * math.pi / den) * (num % den).astype(F32)
        return jnp.cos(ang), jnp.sin(ang)

    i1 = jnp.arange(n1)
    c, s = cs(i1[:, None] * i1[None, :], n1)
    f1_data = jnp.concatenate([jnp.concatenate([c[:, :k1h], s[:, :k1h]], 1),
                               jnp.concatenate([-s[:, :k1h], c[:, :k1h]], 1)], 0)
    f1_real = jnp.concatenate([c, -s], 0)
    ci, si = c[:k1h, :] / n, s[:k1h, :] / n
    f1_inv = jnp.concatenate([jnp.concatenate([ci, -si], 1),
                              jnp.concatenate([si, ci], 1)], 0)
    i2 = jnp.arange(n2)
    num = i2[None, None, :] * (i1[:, None, None] + n1 * i2[None, :, None])
    c2, s2 = cs(num, n)
    g = jnp.concatenate([jnp.concatenate([c2, s2], 2),
                         jnp.concatenate([-s2, c2], 2)], 1)
    h = jnp.swapaxes(g, 1, 2)
    return (f1_data.astype(BF16), f1_real.astype(BF16), f1_inv.astype(BF16),
            g.astype(BF16), h.astype(BF16))


def _lane_blocks_to_rows(x, group):
    ct = x.shape[1] // group
    return jnp.swapaxes(jnp.stack([x[:, j * ct:(j + 1) * ct] for j in range(group)]), 0, 1)


def _rows_to_lane_blocks(x3):
    xt = jnp.swapaxes(x3, 0, 1)
    return jnp.concatenate([xt[j] for j in range(xt.shape[0])], axis=1)


def _dft_stage1(x_ref, f1, t_ref, group):
    def body(i, carry):
        rows = pl.ds(pl.multiple_of(i * group, group), group)
        z = _rows_to_lane_blocks(x_ref[:, rows, :]).astype(BF16)
        t_ref[:, rows, :] = _lane_blocks_to_rows(_dot(f1, z), group).astype(t_ref.dtype)
        return carry

    lax.fori_loop(0, DFT_N2 // group, body, 0)


def _stage2_operand(t_ref, q, n1):
    return jnp.concatenate([t_ref[q], t_ref[n1 + q]], axis=0)


def _filt_mlp_kernel(z_ref, w1_ref, b1_ref, w2_ref, b2_ref, w3_ref, b3_ref, fr_ref, o_ref):
    fr = fr_ref[...]
    h = jnp.sin(fr * (_dot(z_ref[...], w1_ref[...], precision=HIGHEST) + b1_ref[...]))
    h = jnp.sin(fr * (_dot(h, w2_ref[...], precision=HIGHEST) + b2_ref[...]))
    o_ref[...] = jnp.sin(fr * (_dot(h, w3_ref[...], precision=HIGHEST) + b3_ref[...]))


def _filt_spec_kernel(seq, n1, group1, group2, h_ref, w4f_ref, w4b_ref, t_ref, dl_ref, f1_ref, g_ref,
                      o_ref, k_ref, r_ref):
    n2 = DFT_N2
    n = 2 * seq
    ct = o_ref.shape[2]
    row = lax.broadcasted_iota(jnp.int32, (n, ct), 0)
    hk = jnp.concatenate([_dot(h_ref[0:seq, :], w4f_ref[...], precision=HIGHEST),
                          _dot(h_ref[seq:n, :], w4b_ref[...], precision=HIGHEST)], axis=0)
    kraw = jnp.where(row == seq, 0.0, hk * jnp.exp(-t_ref[...] * dl_ref[...]))
    k_ref[...] = (kraw / (jnp.sum(jnp.abs(kraw), axis=0, keepdims=True) + EPS)).reshape(n1, n2, ct)

    _dft_stage1(k_ref, f1_ref[...], r_ref, group1)

    def stage2(i, carry):
        q0 = i * group2
        ops = [_stage2_operand(r_ref, q0 + j, n1) for j in range(group2)]
        for j in range(group2):
            o_ref[q0 + j] = _dot(g_ref[q0 + j], ops[j]).astype(o_ref.dtype)
        return carry

    lax.fori_loop(0, n1 // group2, stage2, 0)


def _hyena_filter_spectrum(seq, d_hy, filt, tables, ct):
    w1, b1, w2, b2, w3, b3, w4, freq = filt
    f1_real, g = tables
    n = 2 * seq
    n2 = DFT_N2
    n1 = n // n2
    order = w1.shape[1]
    pad = LANES - order
    bands = (FILTER_EMB - 1) // 2
    t = jnp.linspace(0.0, 1.0, seq, dtype=F32)[:, None]
    w = 2.0 * math.pi * jnp.arange(seq, dtype=F32)[:, None] / seq
    f = jnp.linspace(1e-4, bands - 1, bands, dtype=F32)[None, :]
    z = jnp.concatenate([t, jnp.cos(f * w), -jnp.sin(f * w)], axis=-1)
    def circular(a):
        return jnp.concatenate([a, a[:1], a[:0:-1]], axis=0)

    zp = jnp.pad(z, ((0, 0), (0, LANES - FILTER_EMB)))
    t_ext = circular(t)
    deltas = jnp.abs(jnp.linspace(math.log(DECAY_TARGET) / SLOW_DECAY_PCT,
                                  math.log(DECAY_TARGET) / FAST_DECAY_PCT, d_hy, dtype=F32))[None, :]

    def padc(a):
        return jnp.pad(a, ((0, 0), (0, pad)))

    w1p = jnp.pad(w1, ((0, LANES - FILTER_EMB), (0, pad)))
    w2p = jnp.pad(w2, ((0, pad), (0, pad)))
    w3p = jnp.pad(w3, ((0, pad), (0, pad)))
    w4p = jnp.pad(w4, ((0, pad), (0, 0)))
    h3 = pl.pallas_call(
        _filt_mlp_kernel,
        out_shape=jax.ShapeDtypeStruct((seq, LANES), F32),
        compiler_params=pltpu.CompilerParams(vmem_limit_bytes=VMEM_LIMIT),
        name="filt_mlp",
    )(zp, w1p, padc(b1[None]), w2p, padc(b2[None]), w3p, padc(b3[None]), padc(freq[None]))
    h3 = circular(h3)

    nct = d_hy // ct
    return pl.pallas_call(
        functools.partial(_filt_spec_kernel, seq, n1, DFT_GROUP, min(STAGE2_GROUP, n1)),
        grid=(nct,),
        in_specs=[pl.BlockSpec((n, LANES), lambda j: (0, 0)),
                  pl.BlockSpec((LANES, ct), lambda j: (0, j)),
                  pl.BlockSpec((LANES, ct), lambda j: (0, nct + j)),
                  pl.BlockSpec((n, 1), lambda j: (0, 0)),
                  pl.BlockSpec((1, ct), lambda j: (0, j)),
                  pl.BlockSpec(f1_real.shape, lambda j: (0, 0)),
                  pl.BlockSpec(g.shape, lambda j: (0, 0, 0))],
        out_specs=pl.BlockSpec((n1, 2 * n2, ct), lambda j: (0, 0, j)),
        out_shape=jax.ShapeDtypeStruct((n1, 2 * n2, d_hy), BF16),
        scratch_shapes=[pltpu.VMEM((n1, n2, ct), F32), pltpu.VMEM((2 * n1, n2, ct), BF16)],
        compiler_params=_params(("arbitrary",)),
        name="filt_spec",
    )(h3, w4p, w4p, t_ext, deltas, f1_real, g)


def _hyena_kernel(seq, n1, group1, group2, x0_ref, uin_ref, bias_ref, f1_ref, f1i_ref, g_ref, h_ref,
                  kf_ref, o_ref, u_ref, r_ref):
    n2 = DFT_N2
    k1h = n1 // 2
    ct = o_ref.shape[2]

    for bb in range(2):
        u_ref[bb * k1h:(bb + 1) * k1h] = uin_ref[bb].astype(F32).reshape(k1h, n2, ct)

    _dft_stage1(u_ref, f1_ref[...], r_ref, group1)

    def stage2(i, carry):
        q0 = i * group2
        ops = [_stage2_operand(r_ref, q0 + j, n1) for j in range(group2)]
        xs = [_dot(g_ref[q0 + j], ops[j]) for j in range(group2)]
        ys = []
        for j in range(group2):
            kf = kf_ref[q0 + j].astype(F32)
            xr, xi, kr, ki = xs[j][:n2], xs[j][n2:], kf[:n2], kf[n2:]
            ys.append(jnp.concatenate([xr * kr - xi * ki, xr * ki + xi * kr], axis=0).astype(BF16))
        outs = [_dot(h_ref[q0 + j], ys[j]) for j in range(group2)]
        for j in range(group2):
            r_ref[q0 + j] = outs[j][:n2].astype(r_ref.dtype)
            r_ref[n1 + q0 + j] = outs[j][n2:].astype(r_ref.dtype)
        return carry

    lax.fori_loop(0, n1 // group2, stage2, 0)

    f1i = f1i_ref[...]
    bias = bias_ref[...]

    def stage3(i, carry):
        rows = pl.ds(pl.multiple_of(i * group1, group1), group1)
        b = _rows_to_lane_blocks(r_ref[:, rows, :].astype(F32)).astype(BF16)
        y = _lane_blocks_to_rows(_dot(f1i, b), group1)
        u_ref[:, rows, :] = y + u_ref[:, rows, :] * bias
        return carry

    lax.fori_loop(0, n2 // group1, stage3, 0)

    for bb in range(2):
        y = u_ref[bb * k1h:(bb + 1) * k1h].reshape(seq, ct)
        o_ref[bb] = (y * x0_ref[bb].astype(F32)).astype(BF16)


def _hyena(p, hy_bias, kf, tables, d_hy, ct):
    b, seq, _ = p.shape
    f1_data, f1_inv, g, h = tables
    n2 = DFT_N2
    n1 = 2 * seq // n2
    nct = d_hy // ct
    once = pl.Buffered(1)
    zspec = lambda off: pl.BlockSpec((2, seq, ct), lambda j, q: (q, 0, off * nct + j))
    return pl.pallas_call(
        functools.partial(_hyena_kernel, seq, n1, DFT_GROUP, min(STAGE2_GROUP, n1)),
        grid=(nct, b // 2),
        in_specs=[zspec(0), zspec(1),
                  pl.BlockSpec((1, ct), lambda j, q: (0, j)),
                  pl.BlockSpec(f1_data.shape, lambda j, q: (0, 0)),
                  pl.BlockSpec(f1_inv.shape, lambda j, q: (0, 0)),
                  pl.BlockSpec(g.shape, lambda j, q: (0, 0, 0), pipeline_mode=once),
                  pl.BlockSpec(h.shape, lambda j, q: (0, 0, 0), pipeline_mode=once),
                  pl.BlockSpec((n1, 2 * n2, ct), lambda j, q: (0, 0, j), pipeline_mode=once)],
        out_specs=pl.BlockSpec((2, seq, ct), lambda j, q: (q, 0, j)),
        out_shape=jax.ShapeDtypeStruct((b, seq, d_hy), BF16),
        scratch_shapes=[pltpu.VMEM((n1, n2, ct), F32), pltpu.VMEM((2 * n1, n2, ct), BF16)],
        compiler_params=_params(("arbitrary", "arbitrary")),
        name="hyena",
    )(p, p, hy_bias.reshape(1, -1), f1_data, f1_inv, g, h, kf)


def _attn_kernel(lam_init, chunks, tq, q_ref, k_ref, v_ref, kc_ref, vc_ref, lam_ref, g_ref, o_ref, vx_ref):
    seq = k_ref.shape[1]
    lctx = kc_ref.shape[1]
    hw = q_ref.shape[2]
    nt = (((1,), (1,)), ((), ()))

    vx_ref[0:seq, 0:hw] = v_ref[0]
    vx_ref[seq:seq + lctx, 0:hw] = vc_ref[0]
    vx_ref[:, hw:2 * hw] = jnp.ones((seq + lctx, hw), BF16)

    lp = lam_ref[...]
    lam = (jnp.exp(jnp.sum(lp[0:1] * lp[1:2], axis=-1, keepdims=True))
           - jnp.exp(jnp.sum(lp[2:3] * lp[3:4], axis=-1, keepdims=True)) + lam_init)

    def q_tile(i, carry):
        rows = pl.ds(pl.multiple_of(i * tq, tq), tq)
        q = q_ref[0, rows, :]
        lane = lax.broadcasted_iota(jnp.int32, q.shape, 1)
        zero = jnp.zeros_like(q)
        qq = jnp.concatenate([jnp.where(lane < HEAD_DIM, q, zero), jnp.where(lane >= HEAD_DIM, q, zero)],
                             axis=0)

        def scores(lo, hi):
            parts = []
            if lo < seq:
                parts.append(lax.dot_general(qq, k_ref[0, lo:min(hi, seq), :], nt, preferred_element_type=F32))
            if hi > seq:
                parts.append(lax.dot_general(qq, kc_ref[0, max(lo, seq) - seq:hi - seq, :], nt,
                                             preferred_element_type=F32))
            return parts[0] if len(parts) == 1 else jnp.concatenate(parts, axis=1)

        m = acc = None
        s_next = scores(*chunks[0])
        for ci, (lo, hi) in enumerate(chunks):
            s = s_next
            if ci + 1 < len(chunks):
                s_next = scores(*chunks[ci + 1])
            mc = jnp.max(s, axis=-1, keepdims=True)
            m_new = mc if m is None else jnp.maximum(m, mc)
            pv = _dot(jnp.exp2(s - m_new).astype(BF16), vx_ref[lo:hi, :])
            acc = pv if acc is None else acc * jnp.exp2(m - m_new) + pv
            m = m_new

        o12 = acc[:, :hw] / acc[:, hw:]
        o = o12[:tq] - lam * o12[tq:]
        o_ref[0, rows, :] = (_rms(o, g_ref[...]) * (1.0 - lam_init)).astype(BF16)
        return carry

    lax.fori_loop(0, seq // tq, q_tile, 0, unroll=4)


def _attn(p, pc, lam_rows, subln_g, lam_init, q_col0, tq, tk):
    b, seq, _ = p.shape
    lctx = pc.shape[1]
    hw = 2 * HEAD_DIM
    c0 = q_col0 // hw
    nh = N_HEADS
    edges = list(range(0, seq, tk)) + [seq + lctx]
    chunks = tuple(zip(edges[:-1], edges[1:]))
    return pl.pallas_call(
        functools.partial(_attn_kernel, lam_init, chunks, tq),
        grid=(b, nh),
        in_specs=[pl.BlockSpec((1, seq, hw), lambda bi, hh: (bi, 0, c0 + hh)),
                  pl.BlockSpec((1, seq, hw), lambda bi, hh: (bi, 0, c0 + nh + hh)),
                  pl.BlockSpec((1, seq, hw), lambda bi, hh: (bi, 0, c0 + 2 * nh + hh)),
                  pl.BlockSpec((1, lctx, hw), lambda bi, hh: (bi, 0, hh)),
                  pl.BlockSpec((1, lctx, hw), lambda bi, hh: (bi, 0, nh + hh)),
                  pl.BlockSpec(lam_rows.shape, lambda bi, hh: (0, 0)),
                  pl.BlockSpec((1, hw), lambda bi, hh: (0, 0))],
        out_specs=pl.BlockSpec((1, seq, hw), lambda bi, hh: (bi, 0, hh)),
        out_shape=jax.ShapeDtypeStruct((b, seq, nh * hw), BF16),
        scratch_shapes=[pltpu.VMEM((seq + lctx, 2 * hw), BF16)],
        compiler_params=_params(("parallel", "parallel")),
        name="attn",
    )(p, p, p, pc, pc, lam_rows, subln_g.reshape(1, hw))


def _merge_kernel(mrow, grow, s_ref, yh_ref, od_ref, gh_ref, gd_ref, mod_ref, g_ref,
                  why_ref, wda_ref, wo_ref, o_ref):
    y_hy = _dot(yh_ref[0], why_ref[...])
    y_da = _dot(od_ref[0], wda_ref[...])
    mix = gh_ref[0].astype(F32) * y_hy + gd_ref[0].astype(F32) * y_da
    y = _dot(mix.astype(BF16), wo_ref[...])
    o_ref[0] = s_ref[0] + mod_ref[0, mrow:mrow + 1, :] * _rms(y, g_ref[grow:grow + 1, :])


def _merge(s, yh, od, p, mod, g, w_hy_out, w_da_out, w_o, *, mrow, grow, gate_chunk0, tm):
    b, t, d = s.shape
    tok = lambda c: pl.BlockSpec((1, tm, d), lambda bi, i: (bi, i, c))
    wfull = lambda w: pl.BlockSpec(w.shape, lambda bi, i: (0, 0))
    return pl.pallas_call(
        functools.partial(_merge_kernel, mrow, grow),
        grid=(b, t // tm),
        in_specs=[tok(0), tok(0), tok(0), tok(gate_chunk0), tok(gate_chunk0 + 1),
                  pl.BlockSpec((1, N_MOD, d), lambda bi, i: (bi, 0, 0)),
                  pl.BlockSpec(g.shape, lambda bi, i: (0, 0)),
                  wfull(w_hy_out), wfull(w_da_out), wfull(w_o)],
        out_specs=tok(0),
        out_shape=jax.ShapeDtypeStruct((b, t, d), F32),
        compiler_params=_params(("parallel", "parallel")),
        name="merge",
    )(s, yh, od, p, p, mod, g, w_hy_out, w_da_out, w_o)


def kernel(x, c, ctx, c_ctx, w_ada, b_ada, norm_g, w_ff_in, w_ff_out, w_in, hy_conv_w, hy_conv_b, filt_w1, filt_b1, filt_w2, filt_b2, filt_w3, filt_b3, filt_w4, filt_freq, hy_bias, lambda_q1, lambda_k1, lambda_q2, lambda_k2, subln_g, w_hy_out, w_da_out, w_o):
    b, seq, d = x.shape
    lctx = ctx.shape[1]
    depth = w_ada.shape[0]
    d_hy = hy_bias.shape[1]
    assert depth == 1 and b % 2 == 0 and b < 16
    ctx_row = b
    tm = min(512, seq)
    tq = min(256, seq)
    tk = min(1536, seq)
    ct = LANES

    rope = _rope_tables(seq)
    f1_data, f1_real, f1_inv, dft_g, dft_h = _dft_tables(seq)
    c16 = jnp.zeros((16, d), F32).at[:b].set(c).at[ctx_row].set(c_ctx)
    latent_row = lambda bi: bi
    context_row = lambda bi: ctx_row
    assert d_hy == d
    modes = ("conv_gate", "conv_a", "conv_b", "rope_q", "rope_k", "plain", "sigmoid", "sigmoid")
    kv_chunk0 = modes.index("rope_k")
    q_col0, gate_chunk0 = 2 * d, 5

    xs, cs = x, ctx
    for l in range(depth):
        lam_init = 0.8 - 0.6 * math.exp(-0.3 * l)
        mod = _ada(c16, w_ada[l], b_ada[l]).reshape(16, N_MOD, d)
        g = norm_g[l]
        wfi = w_ff_in[l].astype(BF16)
        wfo = w_ff_out[l].astype(BF16)
        wi = w_in[l].astype(BF16)
        filt = (filt_w1[l], filt_b1[l], filt_w2[l], filt_b2[l], filt_w3[l], filt_b3[l],
                filt_w4[l], filt_freq[l])
        lam_rows = jnp.stack([lambda_q1[l], lambda_k1[l], lambda_q2[l], lambda_k2[l]])

        xs = _ffn(xs, mod, g, wfi, wfo, half=0, mrow=0, grow=0, mod_row_of_batch=latent_row, tm=tm)
        cs = _ffn(cs, mod, g, wfi, wfo, half=0, mrow=0, grow=0, mod_row_of_batch=context_row,
                  tm=min(tm, lctx))

        p = _proj(xs, mod, g, wi, modes=modes, col_chunk0=0, mrow=3, grow=2,
                  mod_row_of_batch=latent_row, tm=tm, rope=rope,
                  conv=(hy_conv_w[l], hy_conv_b[l].reshape(1, -1)))
        pc = _proj(cs, mod, g, wi, modes=("plain", "plain"), col_chunk0=kv_chunk0, mrow=3, grow=2,
                   mod_row_of_batch=context_row, tm=min(tm, lctx))

        kf = _hyena_filter_spectrum(seq, d_hy, filt, (f1_real, dft_g), ct)
        yh = _hyena(p, hy_bias[l], kf, (f1_data, f1_inv, dft_g, dft_h), d_hy, ct)
        od = _attn(p, pc, lam_rows, subln_g[l], lam_init, q_col0, tq, tk)
        xs = _merge(xs, yh, od, p, mod, g, w_hy_out[l].astype(BF16), w_da_out[l].astype(BF16),
                    w_o[l].astype(BF16), mrow=5, grow=3, gate_chunk0=gate_chunk0, tm=tm)
        xs = _ffn(xs, mod, g, wfi, wfo, half=1, mrow=6, grow=4, mod_row_of_batch=latent_row, tm=tm)
    return xs
```

```python
import functools
import math

import jax
import jax.numpy as jnp
from jax import lax
from jax.experimental import pallas as pl
from jax.experimental.pallas import tpu as pltpu

F32 = jnp.float32
BF16 = jnp.bfloat16
HIGHEST = lax.Precision.HIGHEST

N_HEADS = 8
HEAD_DIM = 64
GRID_W = 64
ROPE_THETA = 10000.0
FILTER_EMB = 33
DECAY_TARGET = 1e-2
FAST_DECAY_PCT = 0.3
SLOW_DECAY_PCT = 1.5
EPS = 1e-6
N_MOD = 9
Q_SCALE = HEAD_DIM ** -0.5 * math.log2(math.e)

LANES = 128
MXU_DIM = 256
DFT_N2 = 128
HALO = 16
DFT_GROUP = 16
STAGE2_GROUP = 16
VMEM_LIMIT = 56 * 1024 * 1024


def _params(sem, vmem=VMEM_LIMIT):
    return pltpu.CompilerParams(dimension_semantics=sem, vmem_limit_bytes=vmem)


def _rms(x, g):
    return x * lax.rsqrt(jnp.mean(x * x, axis=-1, keepdims=True) + EPS) * g


def _dot(a, b, **kw):
    return jnp.dot(a, b, preferred_element_type=F32, **kw)


def _norm_modulate(x, mod_ref, g_ref, mrow, grow, keep=None):
    shift = mod_ref[0, mrow:mrow + 1, :]
    scale = mod_ref[0, mrow + 1:mrow + 2, :]
    y = _rms(x, g_ref[grow:grow + 1, :]) * (1.0 + scale) + shift
    return (y if keep is None else y * keep).astype(BF16)


def _ada_kernel(c_ref, w_ref, b_ref, o_ref):
    c = c_ref[...]
    a = c * jax.nn.sigmoid(c)
    o_ref[...] = _dot(a, w_ref[...], precision=HIGHEST) + b_ref[...]


def _ada(c16, w_ada, b_ada):
    d = c16.shape[1]
    n = w_ada.shape[1]
    return pl.pallas_call(
        _ada_kernel,
        grid=(n // d,),
        in_specs=[pl.BlockSpec((16, d), lambda j: (0, 0)),
                  pl.BlockSpec((d, d), lambda j: (0, j)),
                  pl.BlockSpec((1, d), lambda j: (0, j))],
        out_specs=pl.BlockSpec((16, d), lambda j: (0, j)),
        out_shape=jax.ShapeDtypeStruct((16, n), F32),
        compiler_params=_params(("arbitrary",)),
        name="ada",
    )(c16, w_ada, b_ada.reshape(1, n))


def _ffn_kernel(mrow, grow, bounds, s_ref, mod_ref, g_ref, wi_ref, wo_ref, o_ref):
    dff = wo_ref.shape[0]
    x = s_ref[0]
    xn = _norm_modulate(x, mod_ref, g_ref, mrow, grow)
    acc = None
    for lo, hi in zip(bounds[:-1], bounds[1:]):
        hg = _dot(xn, wi_ref[:, lo:hi])
        hu = _dot(xn, wi_ref[:, dff + lo:dff + hi])
        h = (hg * jax.nn.sigmoid(hg) * hu).astype(BF16)
        part = _dot(h, wo_ref[lo:hi, :])
        acc = part if acc is None else acc + part
    gate = mod_ref[0, mrow + 2:mrow + 3, :]
    o_ref[0] = x + 0.5 * gate * _rms(acc, g_ref[grow + 1:grow + 2, :])


def _ffn(s, mod, g, w_in, w_out, *, half, mrow, grow, mod_row_of_batch, tm):
    b, t, d = s.shape
    dff = w_out.shape[1]
    split = (dff // MXU_DIM + 1) // 2 * MXU_DIM
    bounds = (0, split, dff)
    once = pl.Buffered(1)
    return pl.pallas_call(
        functools.partial(_ffn_kernel, mrow, grow, bounds),
        grid=(b, t // tm),
        in_specs=[pl.BlockSpec((1, tm, d), lambda bi, i: (bi, i, 0)),
                  pl.BlockSpec((1, N_MOD, d), lambda bi, i: (mod_row_of_batch(bi), 0, 0)),
                  pl.BlockSpec(g.shape, lambda bi, i: (0, 0)),
                  pl.BlockSpec((None,) + w_in.shape[1:], lambda bi, i: (half, 0, 0), pipeline_mode=once),
                  pl.BlockSpec((None,) + w_out.shape[1:], lambda bi, i: (half, 0, 0), pipeline_mode=once)],
        out_specs=pl.BlockSpec((1, tm, d), lambda bi, i: (bi, i, 0)),
        out_shape=jax.ShapeDtypeStruct((b, t, d), F32),
        compiler_params=_params(("parallel", "parallel")),
        name="ffn",
    )(s, mod, g, w_in, w_out)


def _rope(acc, cos, sa, sb):
    cols = []
    for k in range(acc.shape[1] // LANES):
        blk = acc[:, k * LANES:(k + 1) * LANES]
        cols.append(blk * cos + pltpu.roll(blk, 16, 1) * sa + pltpu.roll(blk, LANES - 16, 1) * sb)
    return jnp.concatenate(cols, axis=1)


def _proj_kernel(modes, mrow, grow, x_ref, mod_ref, g_ref, w_ref, *rest):
    o_ref = rest[-1]
    d, tm = x_ref.shape[2], x_ref.shape[1]
    xn = _norm_modulate(x_ref[0], mod_ref, g_ref, mrow, grow)
    conv = any(m.startswith("conv") for m in modes)
    if conv:
        prev_ref, next_ref, cw_ref, cb_ref = rest[:4]
        rest = rest[4:]
        i = pl.program_id(1)
        keep_prev = (i > 0).astype(F32)
        keep_next = (i < pl.num_programs(1) - 1).astype(F32)
        xn_ext = jnp.concatenate(
            [_norm_modulate(prev_ref[0], mod_ref, g_ref, mrow, grow, keep_prev), xn,
             _norm_modulate(next_ref[0], mod_ref, g_ref, mrow, grow, keep_next)], axis=0)

        gate0, a0, b0 = (modes.index(m) * d for m in ("conv_gate", "conv_a", "conv_b"))

        def conv_matmuls(lo):
            return [_dot(xn_ext, w_ref[:, c0 + lo:c0 + lo + MXU_DIM]) for c0 in (gate0, a0, b0)]

        def conv_epilogue(lo, zs):
            def conv3(z, c0):
                cols = slice(c0 + lo, c0 + lo + MXU_DIM)
                w3 = cw_ref[:, cols]
                y = (pltpu.roll(z, 1, 0) * w3[0:1] + z * w3[1:2]
                     + pltpu.roll(z, z.shape[0] - 1, 0) * w3[2:3])
                return y[HALO:HALO + tm] + cb_ref[:, cols]

            o_ref[0, :, lo:lo + MXU_DIM] = conv3(zs[0], gate0).astype(BF16)
            o_ref[0, :, d + lo:d + lo + MXU_DIM] = (conv3(zs[1], a0) * conv3(zs[2], b0)).astype(BF16)

    def chunk_epilogue(mode, out_j, acc):
        if mode == "sigmoid":
            acc = jax.nn.sigmoid(acc)
        elif mode in ("rope_q", "rope_k"):
            cos_ref, sa_ref, sb_ref = rest[:3]
            acc = _rope(acc, cos_ref[...], sa_ref[...], sb_ref[...])
            if mode == "rope_q":
                acc = acc * Q_SCALE
        o_ref[0, :, out_j * d:(out_j + 1) * d] = acc.astype(BF16)

    def chunk_matmul(jj):
        return _dot(xn, w_ref[:, jj * d:(jj + 1) * d])

    plain_chunks = [(jj, m) for jj, m in enumerate(modes) if not m.startswith("conv")]
    plain = [(functools.partial(chunk_matmul, jj), functools.partial(chunk_epilogue, mode, out_j))
             for out_j, (jj, mode) in enumerate(plain_chunks, 2 if conv else 0)]
    convs = ([(functools.partial(conv_matmuls, lo), functools.partial(conv_epilogue, lo))
              for lo in range(0, d, MXU_DIM)] if conv else [])
    stages = []
    while plain or convs:
        if convs:
            stages.append(convs.pop(0))
        if plain:
            stages.append(plain.pop(0))
    pending = None
    for matmul, epilogue in stages:
        result = matmul()
        if pending is not None:
            pending[0](pending[1])
        pending = (epilogue, result)
    pending[0](pending[1])


def _proj(x, mod, g, w, *, modes, col_chunk0, mrow, grow, mod_row_of_batch, tm, rope=None, conv=None):
    b, t, d = x.shape
    nch = len(modes)
    n_out = nch - sum(m == "conv_a" for m in modes)
    assert col_chunk0 % nch == 0
    in_specs = [pl.BlockSpec((1, tm, d), lambda bi, i: (bi, i, 0)),
                pl.BlockSpec((1, N_MOD, d), lambda bi, i: (mod_row_of_batch(bi), 0, 0)),
                pl.BlockSpec(g.shape, lambda bi, i: (0, 0)),
                pl.BlockSpec((d, nch * d), lambda bi, i: (0, col_chunk0 // nch),
                             pipeline_mode=pl.Buffered(1))]
    args = [x, mod, g, w]
    if conv is not None:
        per_tile = tm // HALO
        conv_w, conv_b = conv
        in_specs += [pl.BlockSpec((1, HALO, d), lambda bi, i: (bi, jnp.maximum(i * per_tile - 1, 0), 0)),
                     pl.BlockSpec((1, HALO, d),
                                  lambda bi, i: (bi, jnp.minimum((i + 1) * per_tile, t // HALO - 1), 0)),
                     pl.BlockSpec(conv_w.shape, lambda bi, i: (0, 0)),
                     pl.BlockSpec(conv_b.shape, lambda bi, i: (0, 0))]
        args += [x, x, conv_w, conv_b]
    if rope is not None:
        in_specs += [pl.BlockSpec((tm, LANES), lambda bi, i: (i, 0))] * 3
        args += list(rope)
    return pl.pallas_call(
        functools.partial(_proj_kernel, modes, mrow, grow),
        grid=(b, t // tm),
        in_specs=in_specs,
        out_specs=pl.BlockSpec((1, tm, n_out * d), lambda bi, i: (bi, i, 0)),
        out_shape=jax.ShapeDtypeStruct((b, t, n_out * d), BF16),
        compiler_params=_params(("parallel", "parallel")),
        name="proj",
    )(*args)


def _rope_tables(seq):
    pos = jnp.arange(seq)
    r = (pos // GRID_W).astype(F32)[:, None]
    col = (pos % GRID_W).astype(F32)[:, None]
    half = HEAD_DIM // 2
    inv = ROPE_THETA ** (-jnp.arange(0, half, 2, dtype=F32) / half)
    cr, sr = jnp.cos(r * inv), jnp.sin(r * inv)
    cc, sc = jnp.cos(col * inv), jnp.sin(col * inv)
    z = jnp.zeros_like(sr)
    cos = jnp.concatenate([cr, cr, cc, cc], axis=1)
    sa = jnp.concatenate([z, sr, z, sc], axis=1)
    sb = jnp.concatenate([-sr, z, -sc, z], axis=1)
    rep = LANES // HEAD_DIM
    return tuple(jnp.tile(t, (1, rep)) for t in (cos, sa, sb))


def _dft_tables(seq):
    n = 2 * seq
    n2 = DFT_N2
    n1 = n // n2
    k1h = n1 // 2

    def cs(num, den):
        ang = (2.0 * math.pi / den) * (num % den).astype(F32)
        return jnp.cos(ang), jnp.sin(ang)

    i1 = jnp.arange(n1)
    c, s = cs(i1[:, None] * i1[None, :], n1)
    f1_data = jnp.concatenate([jnp.concatenate([c[:, :k1h], s[:, :k1h]], 1),
                               jnp.concatenate([-s[:, :k1h], c[:, :k1h]], 1)], 0)
    f1_real = jnp.concatenate([c, -s], 0)
    ci, si = c[:k1h, :] / n, s[:k1h, :] / n
    f1_inv = jnp.concatenate([jnp.concatenate([ci, -si], 1),
                              jnp.concatenate([si, ci], 1)], 0)
    i2 = jnp.arange(n2)
    num = i2[None, None, :] * (i1[:, None, None] + n1 * i2[None, :, None])
    c2, s2 = cs(num, n)
    g = jnp.concatenate([jnp.concatenate([c2, s2], 2),
                         jnp.concatenate([-s2, c2], 2)], 1)
    h = jnp.swapaxes(g, 1, 2)
    return (f1_data.astype(BF16), f1_real.astype(BF16), f1_inv.astype(BF16),
            g.astype(BF16), h.astype(BF16))


def _lane_blocks_to_rows(x, group):
    ct = x.shape[1] // group
    return jnp.swapaxes(jnp.stack([x[:, j * ct:(j + 1) * ct] for j in range(group)]), 0, 1)


def _rows_to_lane_blocks(x3):
    xt = jnp.swapaxes(x3, 0, 1)
    return jnp.concatenate([xt[j] for j in range(xt.shape[0])], axis=1)


def _dft_stage1(x_ref, f1, t_ref, group):
    def body(i, carry):
        rows = pl.ds(pl.multiple_of(i * group, group), group)
        z = _rows_to_lane_blocks(x_ref[:, rows, :]).astype(BF16)
        t_ref[:, rows, :] = _lane_blocks_to_rows(_dot(f1, z), group).astype(t_ref.dtype)
        return carry

    lax.fori_loop(0, DFT_N2 // group, body, 0)


def _stage2_operand(t_ref, q, n1):
    return jnp.concatenate([t_ref[q], t_ref[n1 + q]], axis=0)


def _filt_mlp_kernel(z_ref, w1_ref, b1_ref, w2_ref, b2_ref, w3_ref, b3_ref, fr_ref, o_ref):
    fr = fr_ref[...]
    h = jnp.sin(fr * (_dot(z_ref[...], w1_ref[...], precision=HIGHEST) + b1_ref[...]))
    h = jnp.sin(fr * (_dot(h, w2_ref[...], precision=HIGHEST) + b2_ref[...]))
    o_ref[...] = jnp.sin(fr * (_dot(h, w3_ref[...], precision=HIGHEST) + b3_ref[...]))


def _filt_spec_kernel(seq, n1, group1, group2, h_ref, w4f_ref, w4b_ref, t_ref, dl_ref, f1_ref, g_ref,
                      o_ref, k_ref, r_ref):
    n2 = DFT_N2
    n = 2 * seq
    ct = o_ref.shape[2]
    row = lax.broadcasted_iota(jnp.int32, (n, ct), 0)
    hk = jnp.concatenate([_dot(h_ref[0:seq, :], w4f_ref[...], precision=HIGHEST),
                          _dot(h_ref[seq:n, :], w4b_ref[...], precision=HIGHEST)], axis=0)
    kraw = jnp.where(row == seq, 0.0, hk * jnp.exp(-t_ref[...] * dl_ref[...]))
    k_ref[...] = (kraw / (jnp.sum(jnp.abs(kraw), axis=0, keepdims=True) + EPS)).reshape(n1, n2, ct)

    _dft_stage1(k_ref, f1_ref[...], r_ref, group1)

    def stage2(i, carry):
        q0 = i * group2
        ops = [_stage2_operand(r_ref, q0 + j, n1) for j in range(group2)]
        for j in range(group2):
            o_ref[q0 + j] = _dot(g_ref[q0 + j], ops[j]).astype(o_ref.dtype)
        return carry

    lax.fori_loop(0, n1 // group2, stage2, 0)


def _hyena_filter_spectrum(seq, d_hy, filt, tables, ct):
    w1, b1, w2, b2, w3, b3, w4, freq = filt
    f1_real, g = tables
    n = 2 * seq
    n2 = DFT_N2
    n1 = n // n2
    order = w1.shape[1]
    pad = LANES - order
    bands = (FILTER_EMB - 1) // 2
    t = jnp.linspace(0.0, 1.0, seq, dtype=F32)[:, None]
    w = 2.0 * math.pi * jnp.arange(seq, dtype=F32)[:, None] / seq
    f = jnp.linspace(1e-4, bands - 1, bands, dtype=F32)[None, :]
    z = jnp.concatenate([t, jnp.cos(f * w), -jnp.sin(f * w)], axis=-1)
    def circular(a):
        return jnp.concatenate([a, a[:1], a[:0:-1]], axis=0)

    zp = jnp.pad(z, ((0, 0), (0, LANES - FILTER_EMB)))
    t_ext = circular(t)
    deltas = jnp.abs(jnp.linspace(math.log(DECAY_TARGET) / SLOW_DECAY_PCT,
                                  math.log(DECAY_TARGET) / FAST_DECAY_PCT, d_hy, dtype=F32))[None, :]

    def padc(a):
        return jnp.pad(a, ((0, 0), (0, pad)))

    w1p = jnp.pad(w1, ((0, LANES - FILTER_EMB), (0, pad)))
    w2p = jnp.pad(w2, ((0, pad), (0, pad)))
    w3p = jnp.pad(w3, ((0, pad), (0, pad)))
    w4p = jnp.pad(w4, ((0, pad), (0, 0)))
    h3 = pl.pallas_call(
        _filt_mlp_kernel,
        out_shape=jax.ShapeDtypeStruct((seq, LANES), F32),
        compiler_params=pltpu.CompilerParams(vmem_limit_bytes=VMEM_LIMIT),
        name="filt_mlp",
    )(zp, w1p, padc(b1[None]), w2p, padc(b2[None]), w3p, padc(b3[None]), padc(freq[None]))
    h3 = circular(h3)

    nct = d_hy // ct
    return pl.pallas_call(
        functools.partial(_filt_spec_kernel, seq, n1, DFT_GROUP, min(STAGE2_GROUP, n1)),
        grid=(nct,),
        in_specs=[pl.BlockSpec((n, LANES), lambda j: (0, 0)),
                  pl.BlockSpec((LANES, ct), lambda j: (0, j)),
                  pl.BlockSpec((LANES, ct), lambda j: (0, nct + j)),
                  pl.BlockSpec((n, 1), lambda j: (0, 0)),
                  pl.BlockSpec((1, ct), lambda j: (0, j)),
                  pl.BlockSpec(f1_real.shape, lambda j: (0, 0)),
                  pl.BlockSpec(g.shape, lambda j: (0, 0, 0))],
        out_specs=pl.BlockSpec((n1, 2 * n2, ct), lambda j: (0, 0, j)),
        out_shape=jax.ShapeDtypeStruct((n1, 2 * n2, d_hy), BF16),
        scratch_shapes=[pltpu.VMEM((n1, n2, ct), F32), pltpu.VMEM((2 * n1, n2, ct), BF16)],
        compiler_params=_params(("arbitrary",)),
        name="filt_spec",
    )(h3, w4p, w4p, t_ext, deltas, f1_real, g)


def _hyena_kernel(seq, n1, group1, group2, x0_ref, uin_ref, bias_ref, f1_ref, f1i_ref, g_ref, h_ref,
                  kf_ref, o_ref, u_ref, r_ref):
    n2 = DFT_N2
    k1h = n1 // 2
    ct = o_ref.shape[2]

    for bb in range(2):
        u_ref[bb * k1h:(bb + 1) * k1h] = uin_ref[bb].astype(F32).reshape(k1h, n2, ct)

    _dft_stage1(u_ref, f1_ref[...], r_ref, group1)

    def stage2(i, carry):
        q0 = i * group2
        ops = [_stage2_operand(r_ref, q0 + j, n1) for j in range(group2)]
        xs = [_dot(g_ref[q0 + j], ops[j]) for j in range(group2)]
        ys = []
        for j in range(group2):
            kf = kf_ref[q0 + j].astype(F32)
            xr, xi, kr, ki = xs[j][:n2], xs[j][n2:], kf[:n2], kf[n2:]
            ys.append(jnp.concatenate([xr * kr - xi * ki, xr * ki + xi * kr], axis=0).astype(BF16))
        outs = [_dot(h_ref[q0 + j], ys[j]) for j in range(group2)]
        for j in range(group2):
            r_ref[q0 + j] = outs[j][:n2].astype(r_ref.dtype)
            r_ref[n1 + q0 + j] = outs[j][n2:].astype(r_ref.dtype)
        return carry

    lax.fori_loop(0, n1 // group2, stage2, 0)

    f1i = f1i_ref[...]
    bias = bias_ref[...]

    def stage3(i, carry):
        rows = pl.ds(pl.multiple_of(i * group1, group1), group1)
        b = _rows_to_lane_blocks(r_ref[:, rows, :].astype(F32)).astype(BF16)
        y = _lane_blocks_to_rows(_dot(f1i, b), group1)
        u_ref[:, rows, :] = y + u_ref[:, rows, :] * bias
        return carry

    lax.fori_loop(0, n2 // group1, stage3, 0)

    for bb in range(2):
        y = u_ref[bb * k1h:(bb + 1) * k1h].reshape(seq, ct)
        o_ref[bb] = (y * x0_ref[bb].astype(F32)).astype(BF16)


def _hyena(p, hy_bias, kf, tables, d_hy, ct):
    b, seq, _ = p.shape
    f1_data, f1_inv, g, h = tables
    n2 = DFT_N2
    n1 = 2 * seq // n2
    nct = d_hy // ct
    once = pl.Buffered(1)
    zspec = lambda off: pl.BlockSpec((2, seq, ct), lambda j, q: (q, 0, off * nct + j))
    return pl.pallas_call(
        functools.partial(_hyena_kernel, seq, n1, DFT_GROUP, min(STAGE2_GROUP, n1)),
        grid=(nct, b // 2),
        in_specs=[zspec(0), zspec(1),
                  pl.BlockSpec((1, ct), lambda j, q: (0, j)),
                  pl.BlockSpec(f1_data.shape, lambda j, q: (0, 0)),
                  pl.BlockSpec(f1_inv.shape, lambda j, q: (0, 0)),
                  pl.BlockSpec(g.shape, lambda j, q: (0, 0, 0), pipeline_mode=once),
                  pl.BlockSpec(h.shape, lambda j, q: (0, 0, 0), pipeline_mode=once),
                  pl.BlockSpec((n1, 2 * n2, ct), lambda j, q: (0, 0, j), pipeline_mode=once)],
        out_specs=pl.BlockSpec((2, seq, ct), lambda j, q: (q, 0, j)),
        out_shape=jax.ShapeDtypeStruct((b, seq, d_hy), BF16),
        scratch_shapes=[pltpu.VMEM((n1, n2, ct), F32), pltpu.VMEM((2 * n1, n2, ct), BF16)],
        compiler_params=_params(("arbitrary", "arbitrary")),
        name="hyena",
    )(p, p, hy_bias.reshape(1, -1), f1_data, f1_inv, g, h, kf)


def _attn_kernel(lam_init, chunks, tq, q_ref, k_ref, v_ref, kc_ref, vc_ref, lam_ref, g_ref, o_ref, vx_ref):
    seq = k_ref.shape[1]
    lctx = kc_ref.shape[1]
    hw = q_ref.shape[2]
    nt = (((1,), (1,)), ((), ()))

    vx_ref[0:seq, 0:hw] = v_ref[0]
    vx_ref[seq:seq + lctx, 0:hw] = vc_ref[0]
    vx_ref[:, hw:2 * hw] = jnp.ones((seq + lctx, hw), BF16)

    lp = lam_ref[...]
    lam = (jnp.exp(jnp.sum(lp[0:1] * lp[1:2], axis=-1, keepdims=True))
           - jnp.exp(jnp.sum(lp[2:3] * lp[3:4], axis=-1, keepdims=True)) + lam_init)

    def q_tile(i, carry):
        rows = pl.ds(pl.multiple_of(i * tq, tq), tq)
        q = q_ref[0, rows, :]
        lane = lax.broadcasted_iota(jnp.int32, q.shape, 1)
        zero = jnp.zeros_like(q)
        qq = jnp.concatenate([jnp.where(lane < HEAD_DIM, q, zero), jnp.where(lane >= HEAD_DIM, q, zero)],
                             axis=0)

        def scores(lo, hi):
            parts = []
            if lo < seq:
                parts.append(lax.dot_general(qq, k_ref[0, lo:min(hi, seq), :], nt, preferred_element_type=F32))
            if hi > seq:
                parts.append(lax.dot_general(qq, kc_ref[0, max(lo, seq) - seq:hi - seq, :], nt,
                                             preferred_element_type=F32))
            return parts[0] if len(parts) == 1 else jnp.concatenate(parts, axis=1)

        m = acc = None
        s_next = scores(*chunks[0])
        for ci, (lo, hi) in enumerate(chunks):
            s = s_next
            if ci + 1 < len(chunks):
                s_next = scores(*chunks[ci + 1])
            mc = jnp.max(s, axis=-1, keepdims=True)
            m_new = mc if m is None else jnp.maximum(m, mc)
            pv = _dot(jnp.exp2(s - m_new).astype(BF16), vx_ref[lo:hi, :])
            acc = pv if acc is None else acc * jnp.exp2(m - m_new) + pv
            m = m_new

        o12 = acc[:, :hw] / acc[:, hw:]
        o = o12[:tq] - lam * o12[tq:]
        o_ref[0, rows, :] = (_rms(o, g_ref[...]) * (1.0 - lam_init)).astype(BF16)
        return carry

    lax.fori_loop(0, seq // tq, q_tile, 0, unroll=8)


def _attn(p, pc, lam_rows, subln_g, lam_init, q_col0, tq, tk):
    b, seq, _ = p.shape
    lctx = pc.shape[1]
    hw = 2 * HEAD_DIM
    c0 = q_col0 // hw
    nh = N_HEADS
    edges = list(range(0, seq, tk)) + [seq + lctx]
    chunks = tuple(zip(edges[:-1], edges[1:]))
    return pl.pallas_call(
        functools.partial(_attn_kernel, lam_init, chunks, tq),
        grid=(b, nh),
        in_specs=[pl.BlockSpec((1, seq, hw), lambda bi, hh: (bi, 0, c0 + hh)),
                  pl.BlockSpec((1, seq, hw), lambda bi, hh: (bi, 0, c0 + nh + hh)),
                  pl.BlockSpec((1, seq, hw), lambda bi, hh: (bi, 0, c0 + 2 * nh + hh)),
                  pl.BlockSpec((1, lctx, hw), lambda bi, hh: (bi, 0, hh)),
                  pl.BlockSpec((1, lctx, hw), lambda bi, hh: (bi, 0, nh + hh)),
                  pl.BlockSpec(lam_rows.shape, lambda bi, hh: (0, 0)),
                  pl.BlockSpec((1, hw), lambda bi, hh: (0, 0))],
        out_specs=pl.BlockSpec((1, seq, hw), lambda bi, hh: (bi, 0, hh)),
        out_shape=jax.ShapeDtypeStruct((b, seq, nh * hw), BF16),
        scratch_shapes=[pltpu.VMEM((seq + lctx, 2 * hw), BF16)],
        compiler_params=_params(("parallel", "parallel")),
        name="attn",
    )(p, p, p, pc, pc, lam_rows, subln_g.reshape(1, hw))


def _merge_kernel(mrow, grow, s_ref, yh_ref, od_ref, gh_ref, gd_ref, mod_ref, g_ref,
                  why_ref, wda_ref, wo_ref, o_ref):
    y_hy = _dot(yh_ref[0], why_ref[...])
    y_da = _dot(od_ref[0], wda_ref[...])
    mix = gh_ref[0].astype(F32) * y_hy + gd_ref[0].astype(F32) * y_da
    y = _dot(mix.astype(BF16), wo_ref[...])
    o_ref[0] = s_ref[0] + mod_ref[0, mrow:mrow + 1, :] * _rms(y, g_ref[grow:grow + 1, :])


def _merge(s, yh, od, p, mod, g, w_hy_out, w_da_out, w_o, *, mrow, grow, gate_chunk0, tm):
    b, t, d = s.shape
    tok = lambda c: pl.BlockSpec((1, tm, d), lambda bi, i: (bi, i, c))
    wfull = lambda w: pl.BlockSpec(w.shape, lambda bi, i: (0, 0))
    return pl.pallas_call(
        functools.partial(_merge_kernel, mrow, grow),
        grid=(b, t // tm),
        in_specs=[tok(0), tok(0), tok(0), tok(gate_chunk0), tok(gate_chunk0 + 1),
                  pl.BlockSpec((1, N_MOD, d), lambda bi, i: (bi, 0, 0)),
                  pl.BlockSpec(g.shape, lambda bi, i: (0, 0)),
                  wfull(w_hy_out), wfull(w_da_out), wfull(w_o)],
        out_specs=tok(0),
        out_shape=jax.ShapeDtypeStruct((b, t, d), F32),
        compiler_params=_params(("parallel", "parallel")),
        name="merge",
    )(s, yh, od, p, p, mod, g, w_hy_out, w_da_out, w_o)


def kernel(x, c, ctx, c_ctx, w_ada, b_ada, norm_g, w_ff_in, w_ff_out, w_in, hy_conv_w, hy_conv_b, filt_w1, filt_b1, filt_w2, filt_b2, filt_w3, filt_b3, filt_w4, filt_freq, hy_bias, lambda_q1, lambda_k1, lambda_q2, lambda_k2, subln_g, w_hy_out, w_da_out, w_o):
    b, seq, d = x.shape
    lctx = ctx.shape[1]
    depth = w_ada.shape[0]
    d_hy = hy_bias.shape[1]
    assert depth == 1 and b % 2 == 0 and b < 16
    ctx_row = b
    tm = min(512, seq)
    tq = min(256, seq)
    tk = min(1536, seq)
    ct = LANES

    rope = _rope_tables(seq)
    f1_data, f1_real, f1_inv, dft_g, dft_h = _dft_tables(seq)
    c16 = jnp.zeros((16, d), F32).at[:b].set(c).at[ctx_row].set(c_ctx)
    latent_row = lambda bi: bi
    context_row = lambda bi: ctx_row
    assert d_hy == d
    modes = ("conv_gate", "conv_a", "conv_b", "rope_q", "rope_k", "plain", "sigmoid", "sigmoid")
    kv_chunk0 = modes.index("rope_k")
    q_col0, gate_chunk0 = 2 * d, 5

    xs, cs = x, ctx
    for l in range(depth):
        lam_init = 0.8 - 0.6 * math.exp(-0.3 * l)
        mod = _ada(c16, w_ada[l], b_ada[l]).reshape(16, N_MOD, d)
        g = norm_g[l]
        wfi = w_ff_in[l].astype(BF16)
        wfo = w_ff_out[l].astype(BF16)
        wi = w_in[l].astype(BF16)
        filt = (filt_w1[l], filt_b1[l], filt_w2[l], filt_b2[l], filt_w3[l], filt_b3[l],
                filt_w4[l], filt_freq[l])
        lam_rows = jnp.stack([lambda_q1[l], lambda_k1[l], lambda_q2[l], lambda_k2[l]])

        xs = _ffn(xs, mod, g, wfi, wfo, half=0, mrow=0, grow=0, mod_row_of_batch=latent_row, tm=tm)
        cs = _ffn(cs, mod, g, wfi, wfo, half=0, mrow=0, grow=0, mod_row_of_batch=context_row,
                  tm=min(tm, lctx))

        p = _proj(xs, mod, g, wi, modes=modes, col_chunk0=0, mrow=3, grow=2,
                  mod_row_of_batch=latent_row, tm=tm, rope=rope,
                  conv=(hy_conv_w[l], hy_conv_b[l].reshape(1, -1)))
        pc = _proj(cs, mod, g, wi, modes=("plain", "plain"), col_chunk0=kv_chunk0, mrow=3, grow=2,
                   mod_row_of_batch=context_row, tm=min(tm, lctx))

        kf = _hyena_filter_spectrum(seq, d_hy, filt, (f1_real, dft_g), ct)
        yh = _hyena(p, hy_bias[l], kf, (f1_data, f1_inv, dft_g, dft_h), d_hy, ct)
        od = _attn(p, pc, lam_rows, subln_g[l], lam_init, q_col0, tq, tk)
        xs = _merge(xs, yh, od, p, mod, g, w_hy_out[l].astype(BF16), w_da_out[l].astype(BF16),
                    w_o[l].astype(BF16), mrow=5, grow=3, gate_chunk0=gate_chunk0, tm=tm)
        xs = _ffn(xs, mod, g, wfi, wfo, half=1, mrow=6, grow=4, mod_row_of_batch=latent_row, tm=tm)
    return xs
```

```python
import functools
import math

import jax
import jax.numpy as jnp
from jax import lax
from jax.experimental import pallas as pl
from jax.experimental.pallas import tpu as pltpu

F32 = jnp.float32
BF16 = jnp.bfloat16
HIGHEST = lax.Precision.HIGHEST

N_HEADS = 8
HEAD_DIM = 64
GRID_W = 64
ROPE_THETA = 10000.0
FILTER_EMB = 33
DECAY_TARGET = 1e-2
FAST_DECAY_PCT = 0.3
SLOW_DECAY_PCT = 1.5
EPS = 1e-6
N_MOD = 9
Q_SCALE = HEAD_DIM ** -0.5 * math.log2(math.e)

LANES = 128
MXU_DIM = 256
DFT_N2 = 128
HALO = 16
DFT_GROUP = 16
STAGE2_GROUP = 16
VMEM_LIMIT = 56 * 1024 * 1024


def _params(sem, vmem=VMEM_LIMIT):
    return pltpu.CompilerParams(dimension_semantics=sem, vmem_limit_bytes=vmem)


def _rms(x, g):
    return x * lax.rsqrt(jnp.mean(x * x, axis=-1, keepdims=True) + EPS) * g


def _dot(a, b, **kw):
    return jnp.dot(a, b, preferred_element_type=F32, **kw)


def _norm_modulate(x, mod_ref, g_ref, mrow, grow, keep=None):
    shift = mod_ref[0, mrow:mrow + 1, :]
    scale = mod_ref[0, mrow + 1:mrow + 2, :]
    y = _rms(x, g_ref[grow:grow + 1, :]) * (1.0 + scale) + shift
    return (y if keep is None else y * keep).astype(BF16)


def _ada_kernel(c_ref, w_ref, b_ref, o_ref):
    c = c_ref[...]
    a = c * jax.nn.sigmoid(c)
    o_ref[...] = _dot(a, w_ref[...], precision=HIGHEST) + b_ref[...]


def _ada(c16, w_ada, b_ada):
    d = c16.shape[1]
    n = w_ada.shape[1]
    return pl.pallas_call(
        _ada_kernel,
        grid=(n // d,),
        in_specs=[pl.BlockSpec((16, d), lambda j: (0, 0)),
                  pl.BlockSpec((d, d), lambda j: (0, j)),
                  pl.BlockSpec((1, d), lambda j: (0, j))],
        out_specs=pl.BlockSpec((16, d), lambda j: (0, j)),
        out_shape=jax.ShapeDtypeStruct((16, n), F32),
        compiler_params=_params(("arbitrary",)),
        name="ada",
    )(c16, w_ada, b_ada.reshape(1, n))


def _ffn_kernel(mrow, grow, bounds, s_ref, mod_ref, g_ref, wi_ref, wo_ref, o_ref):
    dff = wo_ref.shape[0]
    x = s_ref[0]
    xn = _norm_modulate(x, mod_ref, g_ref, mrow, grow)
    acc = None
    for lo, hi in zip(bounds[:-1], bounds[1:]):
        hg = _dot(xn, wi_ref[:, lo:hi])
        hu = _dot(xn, wi_ref[:, dff + lo:dff + hi])
        h = (hg * jax.nn.sigmoid(hg) * hu).astype(BF16)
        part = _dot(h, wo_ref[lo:hi, :])
        acc = part if acc is None else acc + part
    gate = mod_ref[0, mrow + 2:mrow + 3, :]
    o_ref[0] = x + 0.5 * gate * _rms(acc, g_ref[grow + 1:grow + 2, :])


def _ffn(s, mod, g, w_in, w_out, *, half, mrow, grow, mod_row_of_batch, tm):
    b, t, d = s.shape
    dff = w_out.shape[1]
    split = (dff // MXU_DIM + 1) // 2 * MXU_DIM
    bounds = (0, split, dff)
    once = pl.Buffered(1)
    return pl.pallas_call(
        functools.partial(_ffn_kernel, mrow, grow, bounds),
        grid=(b, t // tm),
        in_specs=[pl.BlockSpec((1, tm, d), lambda bi, i: (bi, i, 0)),
                  pl.BlockSpec((1, N_MOD, d), lambda bi, i: (mod_row_of_batch(bi), 0, 0)),
                  pl.BlockSpec(g.shape, lambda bi, i: (0, 0)),
                  pl.BlockSpec((None,) + w_in.shape[1:], lambda bi, i: (half, 0, 0), pipeline_mode=once),
                  pl.BlockSpec((None,) + w_out.shape[1:], lambda bi, i: (half, 0, 0), pipeline_mode=once)],
        out_specs=pl.BlockSpec((1, tm, d), lambda bi, i: (bi, i, 0)),
        out_shape=jax.ShapeDtypeStruct((b, t, d), F32),
        compiler_params=_params(("parallel", "parallel")),
        name="ffn",
    )(s, mod, g, w_in, w_out)


def _rope(acc, cos, sa, sb):
    cols = []
    for k in range(acc.shape[1] // LANES):
        blk = acc[:, k * LANES:(k + 1) * LANES]
        cols.append(blk * cos + pltpu.roll(blk, 16, 1) * sa + pltpu.roll(blk, LANES - 16, 1) * sb)
    return jnp.concatenate(cols, axis=1)


def _proj_kernel(modes, mrow, grow, x_ref, mod_ref, g_ref, w_ref, *rest):
    o_ref = rest[-1]
    d, tm = x_ref.shape[2], x_ref.shape[1]
    xn = _norm_modulate(x_ref[0], mod_ref, g_ref, mrow, grow)
    conv = any(m.startswith("conv") for m in modes)
    if conv:
        prev_ref, next_ref, cw_ref, cb_ref = rest[:4]
        rest = rest[4:]
        i = pl.program_id(1)
        keep_prev = (i > 0).astype(F32)
        keep_next = (i < pl.num_programs(1) - 1).astype(F32)
        xn_ext = jnp.concatenate(
            [_norm_modulate(prev_ref[0], mod_ref, g_ref, mrow, grow, keep_prev), xn,
             _norm_modulate(next_ref[0], mod_ref, g_ref, mrow, grow, keep_next)], axis=0)

        gate0, a0, b0 = (modes.index(m) * d for m in ("conv_gate", "conv_a", "conv_b"))

        def conv_matmuls(lo):
            return [_dot(xn_ext, w_ref[:, c0 + lo:c0 + lo + MXU_DIM]) for c0 in (gate0, a0, b0)]

        def conv_epilogue(lo, zs):
            def conv3(z, c0):
                cols = slice(c0 + lo, c0 + lo + MXU_DIM)
                w3 = cw_ref[:, cols]
                y = (pltpu.roll(z, 1, 0) * w3[0:1] + z * w3[1:2]
                     + pltpu.roll(z, z.shape[0] - 1, 0) * w3[2:3])
                return y[HALO:HALO + tm] + cb_ref[:, cols]

            o_ref[0, :, lo:lo + MXU_DIM] = conv3(zs[0], gate0).astype(BF16)
            o_ref[0, :, d + lo:d + lo + MXU_DIM] = (conv3(zs[1], a0) * conv3(zs[2], b0)).astype(BF16)

    def chunk_epilogue(mode, out_j, acc):
        if mode == "sigmoid":
            acc = jax.nn.sigmoid(acc)
        elif mode in ("rope_q", "rope_k"):
            cos_ref, sa_ref, sb_ref = rest[:3]
            acc = _rope(acc, cos_ref[...], sa_ref[...], sb_ref[...])
            if mode == "rope_q":
                acc = acc * Q_SCALE
        o_ref[0, :, out_j * d:(out_j + 1) * d] = acc.astype(BF16)

    def chunk_matmul(jj):
        return _dot(xn, w_ref[:, jj * d:(jj + 1) * d])

    plain_chunks = [(jj, m) for jj, m in enumerate(modes) if not m.startswith("conv")]
    plain = [(functools.partial(chunk_matmul, jj), functools.partial(chunk_epilogue, mode, out_j))
             for out_j, (jj, mode) in enumerate(plain_chunks, 2 if conv else 0)]
    convs = ([(functools.partial(conv_matmuls, lo), functools.partial(conv_epilogue, lo))
              for lo in range(0, d, MXU_DIM)] if conv else [])
    stages = []
    while plain or convs:
        if convs:
            stages.append(convs.pop(0))
        if plain:
            stages.append(plain.pop(0))
    pending = None
    for matmul, epilogue in stages:
        result = matmul()
        if pending is not None:
            pending[0](pending[1])
        pending = (epilogue, result)
    pending[0](pending[1])


def _proj(x, mod, g, w, *, modes, col_chunk0, mrow, grow, mod_row_of_batch, tm, rope=None, conv=None):
    b, t, d = x.shape
    nch = len(modes)
    n_out = nch - sum(m == "conv_a" for m in modes)
    assert col_chunk0 % nch == 0
    in_specs = [pl.BlockSpec((1, tm, d), lambda bi, i: (bi, i, 0)),
                pl.BlockSpec((1, N_MOD, d), lambda bi, i: (mod_row_of_batch(bi), 0, 0)),
                pl.BlockSpec(g.shape, lambda bi, i: (0, 0)),
                pl.BlockSpec((d, nch * d), lambda bi, i: (0, col_chunk0 // nch),
                             pipeline_mode=pl.Buffered(1))]
    args = [x, mod, g, w]
    if conv is not None:
        per_tile = tm // HALO
        conv_w, conv_b = conv
        in_specs += [pl.BlockSpec((1, HALO, d), lambda bi, i: (bi, jnp.maximum(i * per_tile - 1, 0), 0)),
                     pl.BlockSpec((1, HALO, d),
                                  lambda bi, i: (bi, jnp.minimum((i + 1) * per_tile, t // HALO - 1), 0)),
                     pl.BlockSpec(conv_w.shape, lambda bi, i: (0, 0)),
                     pl.BlockSpec(conv_b.shape, lambda bi, i: (0, 0))]
        args += [x, x, conv_w, conv_b]
    if rope is not None:
        in_specs += [pl.BlockSpec((tm, LANES), lambda bi, i: (i, 0))] * 3
        args += list(rope)
    return pl.pallas_call(
        functools.partial(_proj_kernel, modes, mrow, grow),
        grid=(b, t // tm),
        in_specs=in_specs,
        out_specs=pl.BlockSpec((1, tm, n_out * d), lambda bi, i: (bi, i, 0)),
        out_shape=jax.ShapeDtypeStruct((b, t, n_out * d), BF16),
        compiler_params=_params(("parallel", "parallel")),
        name="proj",
    )(*args)


def _rope_tables(seq):
    pos = jnp.arange(seq)
    r = (pos // GRID_W).astype(F32)[:, None]
    col = (pos % GRID_W).astype(F32)[:, None]
    half = HEAD_DIM // 2
    inv = ROPE_THETA ** (-jnp.arange(0, half, 2, dtype=F32) / half)
    cr, sr = jnp.cos(r * inv), jnp.sin(r * inv)
    cc, sc = jnp.cos(col * inv), jnp.sin(col * inv)
    z = jnp.zeros_like(sr)
    cos = jnp.concatenate([cr, cr, cc, cc], axis=1)
    sa = jnp.concatenate([z, sr, z, sc], axis=1)
    sb = jnp.concatenate([-sr, z, -sc, z], axis=1)
    rep = LANES // HEAD_DIM
    return tuple(jnp.tile(t, (1, rep)) for t in (cos, sa, sb))


def _dft_tables(seq):
    n = 2 * seq
    n2 = DFT_N2
    n1 = n // n2
    k1h = n1 // 2

    def cs(num, den):
        ang = (2.0 * math.pi / den) * (num % den).astype(F32)
        return jnp.cos(ang), jnp.sin(ang)

    i1 = jnp.arange(n1)
    c, s = cs(i1[:, None] * i1[None, :], n1)
    f1_data = jnp.concatenate([jnp.concatenate([c[:, :k1h], s[:, :k1h]], 1),
                               jnp.concatenate([-s[:, :k1h], c[:, :k1h]], 1)], 0)
    f1_real = jnp.concatenate([c, -s], 0)
    ci, si = c[:k1h, :] / n, s[:k1h, :] / n
    f1_inv = jnp.concatenate([jnp.concatenate([ci, -si], 1),
                              jnp.concatenate([si, ci], 1)], 0)
    i2 = jnp.arange(n2)
    num = i2[None, None, :] * (i1[:, None, None] + n1 * i2[None, :, None])
    c2, s2 = cs(num, n)
    g = jnp.concatenate([jnp.concatenate([c2, s2], 2),
                         jnp.concatenate([-s2, c2], 2)], 1)
    h = jnp.swapaxes(g, 1, 2)
    return (f1_data.astype(BF16), f1_real.astype(BF16), f1_inv.astype(BF16),
            g.astype(BF16), h.astype(BF16))


def _lane_blocks_to_rows(x, group):
    ct = x.shape[1] // group
    return jnp.swapaxes(jnp.stack([x[:, j * ct:(j + 1) * ct] for j in range(group)]), 0, 1)


def _rows_to_lane_blocks(x3):
    xt = jnp.swapaxes(x3, 0, 1)
    return jnp.concatenate([xt[j] for j in range(xt.shape[0])], axis=1)


def _dft_stage1(x_ref, f1, t_ref, group):
    def body(i, carry):
        rows = pl.ds(pl.multiple_of(i * group, group), group)
        z = _rows_to_lane_blocks(x_ref[:, rows, :].astype(BF16))
        t_ref[:, rows, :] = _lane_blocks_to_rows(_dot(f1, z).astype(t_ref.dtype), group)
        return carry

    lax.fori_loop(0, DFT_N2 // group, body, 0, unroll=2)


def _stage2_operand(t_ref, q, n1):
    return jnp.concatenate([t_ref[q], t_ref[n1 + q]], axis=0)


def _filt_mlp_kernel(z_ref, w1_ref, b1_ref, w2_ref, b2_ref, w3_ref, b3_ref, fr_ref, o_ref):
    fr = fr_ref[...]
    h = jnp.sin(fr * (_dot(z_ref[...], w1_ref[...], precision=HIGHEST) + b1_ref[...]))
    h = jnp.sin(fr * (_dot(h, w2_ref[...], precision=HIGHEST) + b2_ref[...]))
    o_ref[...] = jnp.sin(fr * (_dot(h, w3_ref[...], precision=HIGHEST) + b3_ref[...]))


def _filt_spec_kernel(seq, n1, group1, group2, h_ref, w4f_ref, w4b_ref, t_ref, dl_ref, f1_ref, g_ref,
                      o_ref, k_ref, r_ref):
    n2 = DFT_N2
    n = 2 * seq
    ct = o_ref.shape[2]
    row = lax.broadcasted_iota(jnp.int32, (n, ct), 0)
    hk = jnp.concatenate([_dot(h_ref[0:seq, :], w4f_ref[...], precision=HIGHEST),
                          _dot(h_ref[seq:n, :], w4b_ref[...], precision=HIGHEST)], axis=0)
    kraw = jnp.where(row == seq, 0.0, hk * jnp.exp(-t_ref[...] * dl_ref[...]))
    k_ref[...] = (kraw / (jnp.sum(jnp.abs(kraw), axis=0, keepdims=True) + EPS)).reshape(n1, n2, ct)

    _dft_stage1(k_ref, f1_ref[...], r_ref, group1)

    def stage2(i, carry):
        q0 = i * group2
        ops = [_stage2_operand(r_ref, q0 + j, n1) for j in range(group2)]
        for j in range(group2):
            o_ref[q0 + j] = _dot(g_ref[q0 + j], ops[j]).astype(o_ref.dtype)
        return carry

    lax.fori_loop(0, n1 // group2, stage2, 0)


def _hyena_filter_spectrum(seq, d_hy, filt, tables, ct):
    w1, b1, w2, b2, w3, b3, w4, freq = filt
    f1_real, g = tables
    n = 2 * seq
    n2 = DFT_N2
    n1 = n // n2
    order = w1.shape[1]
    pad = LANES - order
    bands = (FILTER_EMB - 1) // 2
    t = jnp.linspace(0.0, 1.0, seq, dtype=F32)[:, None]
    w = 2.0 * math.pi * jnp.arange(seq, dtype=F32)[:, None] / seq
    f = jnp.linspace(1e-4, bands - 1, bands, dtype=F32)[None, :]
    z = jnp.concatenate([t, jnp.cos(f * w), -jnp.sin(f * w)], axis=-1)
    def circular(a):
        return jnp.concatenate([a, a[:1], a[:0:-1]], axis=0)

    zp = jnp.pad(z, ((0, 0), (0, LANES - FILTER_EMB)))
    t_ext = circular(t)
    deltas = jnp.abs(jnp.linspace(math.log(DECAY_TARGET) / SLOW_DECAY_PCT,
                                  math.log(DECAY_TARGET) / FAST_DECAY_PCT, d_hy, dtype=F32))[None, :]

    def padc(a):
        return jnp.pad(a, ((0, 0), (0, pad)))

    w1p = jnp.pad(w1, ((0, LANES - FILTER_EMB), (0, pad)))
    w2p = jnp.pad(w2, ((0, pad), (0, pad)))
    w3p = jnp.pad(w3, ((0, pad), (0, pad)))
    w4p = jnp.pad(w4, ((0, pad), (0, 0)))
    h3 = pl.pallas_call(
        _filt_mlp_kernel,
        out_shape=jax.ShapeDtypeStruct((seq, LANES), F32),
        compiler_params=pltpu.CompilerParams(vmem_limit_bytes=VMEM_LIMIT),
        name="filt_mlp",
    )(zp, w1p, padc(b1[None]), w2p, padc(b2[None]), w3p, padc(b3[None]), padc(freq[None]))
    h3 = circular(h3)

    nct = d_hy // ct
    return pl.pallas_call(
        functools.partial(_filt_spec_kernel, seq, n1, DFT_GROUP, min(STAGE2_GROUP, n1)),
        grid=(nct,),
        in_specs=[pl.BlockSpec((n, LANES), lambda j: (0, 0)),
                  pl.BlockSpec((LANES, ct), lambda j: (0, j)),
                  pl.BlockSpec((LANES, ct), lambda j: (0, nct + j)),
                  pl.BlockSpec((n, 1), lambda j: (0, 0)),
                  pl.BlockSpec((1, ct), lambda j: (0, j)),
                  pl.BlockSpec(f1_real.shape, lambda j: (0, 0)),
                  pl.BlockSpec(g.shape, lambda j: (0, 0, 0))],
        out_specs=pl.BlockSpec((n1, 2 * n2, ct), lambda j: (0, 0, j)),
        out_shape=jax.ShapeDtypeStruct((n1, 2 * n2, d_hy), BF16),
        scratch_shapes=[pltpu.VMEM((n1, n2, ct), F32), pltpu.VMEM((2 * n1, n2, ct), BF16)],
        compiler_params=_params(("arbitrary",)),
        name="filt_spec",
    )(h3, w4p, w4p, t_ext, deltas, f1_real, g)


def _hyena_kernel(seq, n1, group1, group2, x0_ref, uin_ref, bias_ref, f1_ref, f1i_ref, g_ref, h_ref,
                  kf_ref, o_ref, u_ref, r_ref):
    n2 = DFT_N2
    k1h = n1 // 2
    ct = o_ref.shape[2]

    for bb in range(2):
        u_ref[bb * k1h:(bb + 1) * k1h] = uin_ref[bb].astype(F32).reshape(k1h, n2, ct)

    _dft_stage1(u_ref, f1_ref[...], r_ref, group1)

    def stage2(i, carry):
        q0 = i * group2
        ops = [_stage2_operand(r_ref, q0 + j, n1) for j in range(group2)]
        xs = [_dot(g_ref[q0 + j], ops[j]) for j in range(group2)]
        ys = []
        for j in range(group2):
            kf = kf_ref[q0 + j].astype(F32)
            xr, xi, kr, ki = xs[j][:n2], xs[j][n2:], kf[:n2], kf[n2:]
            ys.append(jnp.concatenate([xr * kr - xi * ki, xr * ki + xi * kr], axis=0).astype(BF16))
        outs = [_dot(h_ref[q0 + j], ys[j]) for j in range(group2)]
        for j in range(group2):
            r_ref[q0 + j] = outs[j][:n2].astype(r_ref.dtype)
            r_ref[n1 + q0 + j] = outs[j][n2:].astype(r_ref.dtype)
        return carry

    lax.fori_loop(0, n1 // group2, stage2, 0)

    f1i = f1i_ref[...]
    bias = bias_ref[...]

    def stage3(i, carry):
        rows = pl.ds(pl.multiple_of(i * group1, group1), group1)
        b = _rows_to_lane_blocks(r_ref[:, rows, :])
        y = _lane_blocks_to_rows(_dot(f1i, b), group1)
        u_ref[:, rows, :] = y + u_ref[:, rows, :] * bias
        return carry

    lax.fori_loop(0, n2 // group1, stage3, 0, unroll=2)

    for bb in range(2):
        y = u_ref[bb * k1h:(bb + 1) * k1h].reshape(seq, ct)
        o_ref[bb] = (y * x0_ref[bb].astype(F32)).astype(BF16)


def _hyena(p, hy_bias, kf, tables, d_hy, ct):
    b, seq, _ = p.shape
    f1_data, f1_inv, g, h = tables
    n2 = DFT_N2
    n1 = 2 * seq // n2
    nct = d_hy // ct
    once = pl.Buffered(1)
    zspec = lambda off: pl.BlockSpec((2, seq, ct), lambda j, q: (q, 0, off * nct + j))
    return pl.pallas_call(
        functools.partial(_hyena_kernel, seq, n1, DFT_GROUP, min(STAGE2_GROUP, n1)),
        grid=(nct, b // 2),
        in_specs=[zspec(0), zspec(1),
                  pl.BlockSpec((1, ct), lambda j, q: (0, j)),
                  pl.BlockSpec(f1_data.shape, lambda j, q: (0, 0)),
                  pl.BlockSpec(f1_inv.shape, lambda j, q: (0, 0)),
                  pl.BlockSpec(g.shape, lambda j, q: (0, 0, 0), pipeline_mode=once),
                  pl.BlockSpec(h.shape, lambda j, q: (0, 0, 0), pipeline_mode=once),
                  pl.BlockSpec((n1, 2 * n2, ct), lambda j, q: (0, 0, j), pipeline_mode=once)],
        out_specs=pl.BlockSpec((2, seq, ct), lambda j, q: (q, 0, j)),
        out_shape=jax.ShapeDtypeStruct((b, seq, d_hy), BF16),
        scratch_shapes=[pltpu.VMEM((n1, n2, ct), F32), pltpu.VMEM((2 * n1, n2, ct), BF16)],
        compiler_params=_params(("arbitrary", "arbitrary")),
        name="hyena",
    )(p, p, hy_bias.reshape(1, -1), f1_data, f1_inv, g, h, kf)


def _attn_kernel(lam_init, chunks, tq, q_ref, k_ref, v_ref, kc_ref, vc_ref, lam_ref, g_ref, o_ref, vx_ref):
    seq = k_ref.shape[1]
    lctx = kc_ref.shape[1]
    hw = q_ref.shape[2]
    nt = (((1,), (1,)), ((), ()))

    vx_ref[0:seq, 0:hw] = v_ref[0]
    vx_ref[seq:seq + lctx, 0:hw] = vc_ref[0]
    vx_ref[:, hw:2 * hw] = jnp.ones((seq + lctx, hw), BF16)

    lp = lam_ref[...]
    lam = (jnp.exp(jnp.sum(lp[0:1] * lp[1:2], axis=-1, keepdims=True))
           - jnp.exp(jnp.sum(lp[2:3] * lp[3:4], axis=-1, keepdims=True)) + lam_init)

    def q_tile(i, carry):
        rows = pl.ds(pl.multiple_of(i * tq, tq), tq)
        q = q_ref[0, rows, :]
        lane = lax.broadcasted_iota(jnp.int32, q.shape, 1)
        zero = jnp.zeros_like(q)
        qq = jnp.concatenate([jnp.where(lane < HEAD_DIM, q, zero), jnp.where(lane >= HEAD_DIM, q, zero)],
                             axis=0)

        def scores(lo, hi):
            parts = []
            if lo < seq:
                parts.append(lax.dot_general(qq, k_ref[0, lo:min(hi, seq), :], nt, preferred_element_type=F32))
            if hi > seq:
                parts.append(lax.dot_general(qq, kc_ref[0, max(lo, seq) - seq:hi - seq, :], nt,
                                             preferred_element_type=F32))
            return parts[0] if len(parts) == 1 else jnp.concatenate(parts, axis=1)

        m = acc = None
        s_next = scores(*chunks[0])
        for ci, (lo, hi) in enumerate(chunks):
            s = s_next
            if ci + 1 < len(chunks):
                s_next = scores(*chunks[ci + 1])
            mc = jnp.max(s, axis=-1, keepdims=True)
            m_new = mc if m is None else jnp.maximum(m, mc)
            pv = _dot(jnp.exp2(s - m_new).astype(BF16), vx_ref[lo:hi, :])
            acc = pv if acc is None else acc * jnp.exp2(m - m_new) + pv
            m = m_new

        o12 = acc[:, :hw] / acc[:, hw:]
        o = o12[:tq] - lam * o12[tq:]
        o_ref[0, rows, :] = (_rms(o, g_ref[...]) * (1.0 - lam_init)).astype(BF16)
        return carry

    lax.fori_loop(0, seq // tq, q_tile, 0, unroll=8)


def _attn(p, pc, lam_rows, subln_g, lam_init, q_col0, tq, tk):
    b, seq, _ = p.shape
    lctx = pc.shape[1]
    hw = 2 * HEAD_DIM
    c0 = q_col0 // hw
    nh = N_HEADS
    edges = list(range(0, seq, tk)) + [seq + lctx]
    chunks = tuple(zip(edges[:-1], edges[1:]))
    return pl.pallas_call(
        functools.partial(_attn_kernel, lam_init, chunks, tq),
        grid=(b, nh),
        in_specs=[pl.BlockSpec((1, seq, hw), lambda bi, hh: (bi, 0, c0 + hh)),
                  pl.BlockSpec((1, seq, hw), lambda bi, hh: (bi, 0, c0 + nh + hh)),
                  pl.BlockSpec((1, seq, hw), lambda bi, hh: (bi, 0, c0 + 2 * nh + hh)),
                  pl.BlockSpec((1, lctx, hw), lambda bi, hh: (bi, 0, hh)),
                  pl.BlockSpec((1, lctx, hw), lambda bi, hh: (bi, 0, nh + hh)),
                  pl.BlockSpec(lam_rows.shape, lambda bi, hh: (0, 0)),
                  pl.BlockSpec((1, hw), lambda bi, hh: (0, 0))],
        out_specs=pl.BlockSpec((1, seq, hw), lambda bi, hh: (bi, 0, hh)),
        out_shape=jax.ShapeDtypeStruct((b, seq, nh * hw), BF16),
        scratch_shapes=[pltpu.VMEM((seq + lctx, 2 * hw), BF16)],
        compiler_params=_params(("parallel", "parallel")),
        name="attn",
    )(p, p, p, pc, pc, lam_rows, subln_g.reshape(1, hw))


def _merge_kernel(mrow, grow, s_ref, yh_ref, od_ref, gh_ref, gd_ref, mod_ref, g_ref,
                  why_ref, wda_ref, wo_ref, o_ref):
    y_hy = _dot(yh_ref[0], why_ref[...])
    y_da = _dot(od_ref[0], wda_ref[...])
    mix = gh_ref[0].astype(F32) * y_hy + gd_ref[0].astype(F32) * y_da
    y = _dot(mix.astype(BF16), wo_ref[...])
    o_ref[0] = s_ref[0] + mod_ref[0, mrow:mrow + 1, :] * _rms(y, g_ref[grow:grow + 1, :])


def _merge(s, yh, od, p, mod, g, w_hy_out, w_da_out, w_o, *, mrow, grow, gate_chunk0, tm):
    b, t, d = s.shape
    tok = lambda c: pl.BlockSpec((1, tm, d), lambda bi, i: (bi, i, c))
    wfull = lambda w: pl.BlockSpec(w.shape, lambda bi, i: (0, 0))
    return pl.pallas_call(
        functools.partial(_merge_kernel, mrow, grow),
        grid=(b, t // tm),
        in_specs=[tok(0), tok(0), tok(0), tok(gate_chunk0), tok(gate_chunk0 + 1),
                  pl.BlockSpec((1, N_MOD, d), lambda bi, i: (bi, 0, 0)),
                  pl.BlockSpec(g.shape, lambda bi, i: (0, 0)),
                  wfull(w_hy_out), wfull(w_da_out), wfull(w_o)],
        out_specs=tok(0),
        out_shape=jax.ShapeDtypeStruct((b, t, d), F32),
        compiler_params=_params(("parallel", "parallel")),
        name="merge",
    )(s, yh, od, p, p, mod, g, w_hy_out, w_da_out, w_o)


def kernel(x, c, ctx, c_ctx, w_ada, b_ada, norm_g, w_ff_in, w_ff_out, w_in, hy_conv_w, hy_conv_b, filt_w1, filt_b1, filt_w2, filt_b2, filt_w3, filt_b3, filt_w4, filt_freq, hy_bias, lambda_q1, lambda_k1, lambda_q2, lambda_k2, subln_g, w_hy_out, w_da_out, w_o):
    b, seq, d = x.shape
    lctx = ctx.shape[1]
    depth = w_ada.shape[0]
    d_hy = hy_bias.shape[1]
    assert depth == 1 and b % 2 == 0 and b < 16
    ctx_row = b
    tm = min(512, seq)
    tq = min(256, seq)
    tk = min(1536, seq)
    ct = LANES

    rope = _rope_tables(seq)
    f1_data, f1_real, f1_inv, dft_g, dft_h = _dft_tables(seq)
    c16 = jnp.zeros((16, d), F32).at[:b].set(c).at[ctx_row].set(c_ctx)
    latent_row = lambda bi: bi
    context_row = lambda bi: ctx_row
    assert d_hy == d
    modes = ("conv_gate", "conv_a", "conv_b", "rope_q", "rope_k", "plain", "sigmoid", "sigmoid")
    kv_chunk0 = modes.index("rope_k")
    q_col0, gate_chunk0 = 2 * d, 5

    xs, cs = x, ctx
    for l in range(depth):
        lam_init = 0.8 - 0.6 * math.exp(-0.3 * l)
        mod = _ada(c16, w_ada[l], b_ada[l]).reshape(16, N_MOD, d)
        g = norm_g[l]
        wfi = w_ff_in[l].astype(BF16)
        wfo = w_ff_out[l].astype(BF16)
        wi = w_in[l].astype(BF16)
        filt = (filt_w1[l], filt_b1[l], filt_w2[l], filt_b2[l], filt_w3[l], filt_b3[l],
                filt_w4[l], filt_freq[l])
        lam_rows = jnp.stack([lambda_q1[l], lambda_k1[l], lambda_q2[l], lambda_k2[l]])

        xs = _ffn(xs, mod, g, wfi, wfo, half=0, mrow=0, grow=0, mod_row_of_batch=latent_row, tm=tm)
        cs = _ffn(cs, mod, g, wfi, wfo, half=0, mrow=0, grow=0, mod_row_of_batch=context_row,
                  tm=min(tm, lctx))

        p = _proj(xs, mod, g, wi, modes=modes, col_chunk0=0, mrow=3, grow=2,
                  mod_row_of_batch=latent_row, tm=tm, rope=rope,
                  conv=(hy_conv_w[l], hy_conv_b[l].reshape(1, -1)))
        pc = _proj(cs, mod, g, wi, modes=("plain", "plain"), col_chunk0=kv_chunk0, mrow=3, grow=2,
                   mod_row_of_batch=context_row, tm=min(tm, lctx))

        kf = _hyena_filter_spectrum(seq, d_hy, filt, (f1_real, dft_g), ct)
        yh = _hyena(p, hy_bias[l], kf, (f1_data, f1_inv, dft_g, dft_h), d_hy, ct)
        od = _attn(p, pc, lam_rows, subln_g[l], lam_init, q_col0, tq, tk)
        xs = _merge(xs, yh, od, p, mod, g, w_hy_out[l].astype(BF16), w_da_out[l].astype(BF16),
                    w_o[l].astype(BF16), mrow=5, grow=3, gate_chunk0=gate_chunk0, tm=tm)
        xs = _ffn(xs, mod, g, wfi, wfo, half=1, mrow=6, grow=4, mod_row_of_batch=latent_row, tm=tm)
    return xs
```
